```python
import jax, jax.numpy as jnp
from jax import lax
import numpy as np

D_MODEL = 1024
BATCH = 32
SEQ = 2048
DEPTH = 4

POOL_WINDOWS = (2, 4, 8, 16)
N_POOL_GROUPS = len(POOL_WINDOWS)
POOL_WIDTH = D_MODEL // 2
POOL_GROUP_DIM = POOL_WIDTH // N_POOL_GROUPS
ATT_HEADS = 8
ATT_KV_HEADS = 2
ATT_GROUP = ATT_HEADS // ATT_KV_HEADS
ATT_HEAD_DIM = 64
ATT_WIDTH = ATT_HEADS * ATT_HEAD_DIM
ATT_KV_WIDTH = ATT_KV_HEADS * ATT_HEAD_DIM
WINDOW = 128
BLOCK = 128
EVEN_IN = 2 * POOL_WIDTH + 2 * ATT_WIDTH + 2 * ATT_KV_WIDTH
MIX_WIDTH_EVEN = POOL_WIDTH + ATT_WIDTH

GLA_HEADS = 4
GLA_KEY_WIDTH = D_MODEL // 2
GLA_VAL_WIDTH = D_MODEL
GLA_DK = GLA_KEY_WIDTH // GLA_HEADS
GLA_DV = GLA_VAL_WIDTH // GLA_HEADS
GLA_GATE_RANK = 16
GLA_GATE_NORMALIZER = 16.0
GLA_CHUNK = 64
ODD_IN = 2 * GLA_KEY_WIDTH + 2 * GLA_VAL_WIDTH + 2 * GLA_GATE_RANK

N_EVEN = (DEPTH + 1) // 2
N_ODD = DEPTH // 2
EPS = 1e-6
NEG = -1e30

kernel_name = 'bidir_pool_swa_gla_hybrid'


def rms_norm(x, w):
    xf = x.astype(jnp.float32)
    y = xf * lax.rsqrt(jnp.mean(xf * xf, axis=-1, keepdims=True) + EPS)
    return (y * w.astype(jnp.float32)).astype(x.dtype)


def split_cols(a, sizes):
    return jnp.split(a, np.cumsum(sizes)[:-1].tolist(), axis=-1)


def pool_mixer(u, w_pool, scale):
    B, S, _ = u.shape
    uf = u.astype(jnp.float32).reshape(B, S, N_POOL_GROUPS, POOL_GROUP_DIM)
    cs = jnp.concatenate([jnp.zeros((B, 1, N_POOL_GROUPS, POOL_GROUP_DIM), jnp.float32),
                          jnp.cumsum(uf, axis=1)], axis=1)
    t = jnp.arange(S)
    means = []
    for g, w in enumerate(POOL_WINDOWS):
        lo = jnp.clip(t - w // 2, 0, S - 1)
        hi = jnp.clip(t + w // 2 - 1, 0, S - 1)
        cs_g = cs[:, :, g]
        tot = jnp.take(cs_g, hi + 1, axis=1) - jnp.take(cs_g, lo, axis=1)
        cnt = (hi - lo + 1).astype(jnp.float32)[None, :, None]
        means.append(tot / cnt)
    pooled = jnp.stack(means, axis=2) - uf
    y = jnp.einsum('bsgc,gcd->bsgd', pooled, w_pool.astype(jnp.float32))
    y = y.reshape(B, S, POOL_WIDTH) * scale.astype(jnp.float32)
    return y.astype(u.dtype)


def windowed_gqa(q, k, v, sink):
    B, S, _, _ = q.shape
    nb = S // BLOCK
    scale = ATT_HEAD_DIM ** -0.5
    slopes = (2.0 ** (-8.0 * jnp.arange(1, ATT_HEADS + 1, dtype=jnp.float32) / ATT_HEADS))
    slopes = slopes.reshape(ATT_KV_HEADS, ATT_GROUP)[None, :, :, None, None]
    sink_b = sink.astype(jnp.float32).reshape(ATT_KV_HEADS, ATT_GROUP)[None, :, :, None, None]
    pad = ((0, 0), (BLOCK, BLOCK), (0, 0), (0, 0))
    kp = jnp.pad(k, pad)
    vp = jnp.pad(v, pad)

    def block(i):
        qi = lax.dynamic_slice_in_dim(q, i * BLOCK, BLOCK, axis=1)
        qi = qi.reshape(B, BLOCK, ATT_KV_HEADS, ATT_GROUP, ATT_HEAD_DIM)
        ki = lax.dynamic_slice_in_dim(kp, i * BLOCK, 3 * BLOCK, axis=1)
        vi = lax.dynamic_slice_in_dim(vp, i * BLOCK, 3 * BLOCK, axis=1)
        s = jnp.einsum('bqkgd,bjkd->bkgqj', qi, ki) * scale
        tq = i * BLOCK + jnp.arange(BLOCK)
        sk = i * BLOCK - BLOCK + jnp.arange(3 * BLOCK)
        dist = jnp.abs(tq[:, None] - sk[None, :])
        valid = (dist <= WINDOW) & (sk >= 0)[None, :] & (sk < S)[None, :]
        s = s - slopes * dist.astype(jnp.float32)
        s = jnp.where(valid, s, NEG)
        m = jnp.maximum(jnp.max(s, axis=-1, keepdims=True), sink_b)
        p = jnp.exp(s - m)
        denom = jnp.sum(p, axis=-1) + jnp.exp(sink_b - m)[..., 0]
        o = jnp.einsum('bkgqj,bjkd->bqkgd', p, vi)
        o = o / jnp.transpose(denom, (0, 3, 1, 2))[..., None]
        return o.reshape(B, BLOCK, ATT_HEADS, ATT_HEAD_DIM)

    out = lax.map(block, jnp.arange(nb))
    return jnp.moveaxis(out, 0, 1).reshape(B, S, ATT_WIDTH)


def gla_direction(q, k, v, g, strict):
    B, S, H, DK = q.shape
    DV = v.shape[-1]
    C = GLA_CHUNK
    nc = S // C
    q = q.reshape(B, nc, C, H, DK)
    k = k.reshape(B, nc, C, H, DK)
    v = v.reshape(B, nc, C, H, DV)
    b = jnp.cumsum(g.reshape(B, nc, C, H, DK), axis=2)
    ref = b[:, :, C // 2:C // 2 + 1]
    b_last = b[:, :, -1]
    att = jnp.einsum('bnihk,bnjhk->bnhij', q * jnp.exp(b - ref), k * jnp.exp(ref - b))
    mask = jnp.tril(jnp.ones((C, C), dtype=bool), k=-1 if strict else 0)
    att = jnp.where(mask, att, 0.0)
    o_intra = jnp.einsum('bnhij,bnjhv->bnihv', att, v)
    q_x = q * jnp.exp(b)
    k_x = k * jnp.exp(b_last[:, :, None] - b)
    decay = jnp.exp(b_last)

    def step(state, xs):
        qc, kc, vc, dc = xs
        o = jnp.einsum('bihk,bhkv->bihv', qc, state)
        state = dc[..., None] * state + jnp.einsum('bjhk,bjhv->bhkv', kc, vc)
        return state, o

    xs = (jnp.moveaxis(q_x, 1, 0), jnp.moveaxis(k_x, 1, 0), jnp.moveaxis(v, 1, 0), jnp.moveaxis(decay, 1, 0))
    _, o_inter = lax.scan(step, jnp.zeros((B, H, DK, DV), jnp.float32), xs)
    o = o_intra + jnp.moveaxis(o_inter, 0, 1)
    return o.reshape(B, S, H, DV)


def even_mixer(h, w_in, w_pool, pool_scale, q_norm_w, k_norm_w, sink, w_out):
    B, S, _ = h.shape
    proj = h @ w_in
    u_pool, z_pool, q, k, v, z_att = split_cols(
        proj, [POOL_WIDTH, POOL_WIDTH, ATT_WIDTH, ATT_KV_WIDTH, ATT_KV_WIDTH, ATT_WIDTH])
    y_pool = pool_mixer(u_pool, w_pool, pool_scale) * jax.nn.silu(z_pool)
    q = rms_norm(q.reshape(B, S, ATT_HEADS, ATT_HEAD_DIM).astype(jnp.float32), q_norm_w)
    k = rms_norm(k.reshape(B, S, ATT_KV_HEADS, ATT_HEAD_DIM).astype(jnp.float32), k_norm_w)
    v = v.reshape(B, S, ATT_KV_HEADS, ATT_HEAD_DIM).astype(jnp.float32)
    y_att = windowed_gqa(q, k, v, sink).astype(h.dtype) * jax.nn.silu(z_att)
    return jnp.concatenate([y_pool, y_att], axis=-1) @ w_out


def odd_mixer(h, w_in, w_gate_up, b_gate, gla_norm_w, w_out):
    B, S, _ = h.shape
    proj = h @ w_in
    q, k, v, z, a_f, a_b = split_cols(
        proj, [GLA_KEY_WIDTH, GLA_KEY_WIDTH, GLA_VAL_WIDTH, GLA_VAL_WIDTH, GLA_GATE_RANK, GLA_GATE_RANK])
    q = q.reshape(B, S, GLA_HEADS, GLA_DK).astype(jnp.float32) * (GLA_DK ** -0.5)
    k = k.reshape(B, S, GLA_HEADS, GLA_DK).astype(jnp.float32)
    v = v.reshape(B, S, GLA_HEADS, GLA_DV).astype(jnp.float32)

    def log_gate(a, d):
        logits = (a @ w_gate_up[d] + b_gate[d]).astype(jnp.float32)
        return (jax.nn.log_sigmoid(logits) / GLA_GATE_NORMALIZER).reshape(B, S, GLA_HEADS, GLA_DK)

    g_f = log_gate(a_f, 0)
    g_b = log_gate(a_b, 1)
    o_f = gla_direction(q, k, v, g_f, False)
    o_b = jnp.flip(gla_direction(jnp.flip(q, 1), jnp.flip(k, 1), jnp.flip(v, 1), jnp.flip(g_b, 1), True), 1)
    o = rms_norm(o_f + o_b, gla_norm_w).reshape(B, S, GLA_VAL_WIDTH)
    return (o.astype(h.dtype) * jax.nn.silu(z)) @ w_out


def setup_inputs(seed: int = 0) -> dict:
    key = jax.random.key(seed)
    ks = jax.random.split(key, 20)
    f32 = jnp.float32
    nrm = lambda k_, shape, s: jax.random.normal(k_, shape, f32) * s
    D = D_MODEL
    return {
        'x': nrm(ks[0], (BATCH, SEQ, D), 1.0),
        'c': nrm(ks[1], (BATCH, D), 1.0),
        'norm_w': 1.0 + nrm(ks[2], (DEPTH, D), 0.02),
        'w_ada': nrm(ks[3], (DEPTH, D, 3 * D), 0.2 * D ** -0.5),
        'b_ada': nrm(ks[4], (DEPTH, 3 * D), 0.02),
        'w_in_a': nrm(ks[5], (N_EVEN, D, EVEN_IN), D ** -0.5),
        'w_pool': nrm(ks[6], (N_EVEN, N_POOL_GROUPS, POOL_GROUP_DIM, POOL_GROUP_DIM), POOL_GROUP_DIM ** -0.5),
        'pool_scale': 1.0 + nrm(ks[7], (N_EVEN, POOL_WIDTH), 0.02),
        'q_norm_w': 1.0 + nrm(ks[8], (N_EVEN, ATT_HEAD_DIM), 0.02),
        'k_norm_w': 1.0 + nrm(ks[9], (N_EVEN, ATT_HEAD_DIM), 0.02),
        'attn_sink': nrm(ks[10], (N_EVEN, ATT_HEADS), 0.5),
        'w_out_a': nrm(ks[11], (N_EVEN, MIX_WIDTH_EVEN, D), MIX_WIDTH_EVEN ** -0.5),
        'w_in_c': nrm(ks[12], (N_ODD, D, ODD_IN), D ** -0.5),
        'w_gate_up': nrm(ks[13], (N_ODD, 2, GLA_GATE_RANK, GLA_KEY_WIDTH), GLA_GATE_RANK ** -0.5),
        'b_gate': nrm(ks[14], (N_ODD, 2, GLA_KEY_WIDTH), 0.01),
        'gla_norm_w': 1.0 + nrm(ks[15], (N_ODD, GLA_DV), 0.02),
        'w_out_c': nrm(ks[16], (N_ODD, GLA_VAL_WIDTH, D), GLA_VAL_WIDTH ** -0.5),
    }


def reference(x, c, norm_w, w_ada, b_ada, w_in_a, w_pool, pool_scale, q_norm_w, k_norm_w,
              attn_sink, w_out_a, w_in_c, w_gate_up, b_gate, gla_norm_w, w_out_c):
    for l in range(DEPTH):
        mod = jax.nn.silu(c) @ w_ada[l] + b_ada[l]
        shift, scale, gate = jnp.split(mod, 3, axis=-1)
        h = rms_norm(x, norm_w[l]) * (1.0 + scale[:, None, :]) + shift[:, None, :]
        if l % 2 == 0:
            i = l // 2
            y = even_mixer(h, w_in_a[i], w_pool[i], pool_scale[i], q_norm_w[i], k_norm_w[i],
                           attn_sink[i], w_out_a[i])
        else:
            j = l // 2
            y = odd_mixer(h, w_in_c[j], w_gate_up[j], b_gate[j], gla_norm_w[j], w_out_c[j])
        x = x + gate[:, None, :] * y
    return x
```

```python
import functools

import jax
import jax.numpy as jnp
from jax import lax
from jax.experimental import pallas as pl
from jax.experimental.pallas import tpu as pltpu

F32 = jnp.float32
BF16 = jnp.bfloat16

D_MODEL = 1024
DEPTH = 4
POOL_WINDOWS = (2, 4, 8, 16)
POOL_WIDTH = 512
POOL_GROUP_DIM = 128
ATT_HEADS = 8
ATT_KV_HEADS = 2
ATT_GROUP = ATT_HEADS // ATT_KV_HEADS
ATT_HEAD_DIM = 64
ATT_WIDTH = 512
ATT_KV_WIDTH = 128
WINDOW = 128
BLOCK = 128
GLA_HEADS = 4
GLA_KEY_WIDTH = 512
GLA_VAL_WIDTH = 1024
GLA_DK = 128
GLA_DV = 256
GLA_GATE_RANK = 16
GLA_GATE_NORMALIZER = 16.0
EPS = 1e-6
NEG = -1e30

LANES = 128
POOL_HALO = 16
GLA_CHUNK = 128
VMEM_LIMIT = 56 * 1024 * 1024


def _dot(a, b):
    return jnp.dot(a, b, preferred_element_type=F32)


def _dot_nt(a, b):
    return lax.dot_general(a, b, (((1,), (1,)), ((), ())), preferred_element_type=F32)


def _dot_tn(a, b):
    return lax.dot_general(a, b, (((0,), (0,)), ((), ())), preferred_element_type=F32)


def _split_bf16(x):
    hi = x.astype(BF16)
    lo = (x - hi.astype(F32)).astype(BF16)
    return hi, lo


def _silu(x):
    return x * jax.nn.sigmoid(x)


def _mod_kernel(c_ref, w_ref, b_ref, o_ref):
    sc = _silu(c_ref[...])
    sc_hi, sc_lo = _split_bf16(sc)
    w_hi, w_lo = _split_bf16(w_ref[0])
    acc = _dot(sc_hi, w_hi) + _dot(sc_lo, w_hi) + _dot(sc_hi, w_lo)
    o_ref[0] = acc + b_ref[0]


def _modulation(c, w_ada, b_ada):
    depth, d, n = w_ada.shape
    bsz = c.shape[0]
    tn = 1024
    return pl.pallas_call(
        _mod_kernel,
        grid=(depth, n // tn),
        in_specs=[
            pl.BlockSpec((bsz, d), lambda l, j: (0, 0)),
            pl.BlockSpec((1, d, tn), lambda l, j: (l, 0, j)),
            pl.BlockSpec((1, 1, tn), lambda l, j: (l, 0, j)),
        ],
        out_specs=pl.BlockSpec((1, bsz, tn), lambda l, j: (l, 0, j)),
        out_shape=jax.ShapeDtypeStruct((depth, bsz, n), F32),
        compiler_params=pltpu.CompilerParams(
            dimension_semantics=("parallel", "parallel"), vmem_limit_bytes=VMEM_LIMIT),
        name="adaln_mod",
    )(c, w_ada, b_ada.reshape(depth, 1, n))


def _norm_modulate(x, nw, mod):
    ms = jnp.mean(x * x, axis=-1, keepdims=True)
    y = (x * lax.rsqrt(ms + EPS)) * nw
    shift = mod[0:1, :]
    scale = mod[1:2, :]
    return (y * (1.0 + scale) + shift).astype(BF16)


def _head_rms(p, w2, head_dim):
    assert 2 * head_dim == LANES and p.shape[-1] == LANES
    lane = lax.broadcasted_iota(jnp.int32, (1, LANES), 1)
    first = lane < head_dim
    sq = p * p
    s0 = jnp.sum(jnp.where(first, sq, 0.0), axis=-1, keepdims=True)
    s1 = jnp.sum(jnp.where(first, 0.0, sq), axis=-1, keepdims=True)
    ms = jnp.where(first, s0, s1) * (1.0 / head_dim)
    return (p * lax.rsqrt(ms + EPS)) * w2


def _even_in_kernel(x_ref, mod_ref, nw_ref, w_ref, qw_ref, kw_ref,
                    u_ref, zp_ref, q_ref, kv_ref, za_ref):
    hb = _norm_modulate(x_ref[0], nw_ref[...], mod_ref[0])
    c0 = 0
    u_ref[0] = _dot(hb, w_ref[:, c0:c0 + POOL_WIDTH]).astype(BF16)
    c0 += POOL_WIDTH
    zp_ref[0] = _silu(_dot(hb, w_ref[:, c0:c0 + POOL_WIDTH])).astype(BF16)
    c0 += POOL_WIDTH
    pq = _dot(hb, w_ref[:, c0:c0 + ATT_WIDTH])
    c0 += ATT_WIDTH
    qscale = ATT_HEAD_DIM ** -0.5
    for j in range(ATT_WIDTH // LANES):
        sl = slice(j * LANES, (j + 1) * LANES)
        q_ref[0, :, sl] = (_head_rms(pq[:, sl], qw_ref[...], ATT_HEAD_DIM) * qscale).astype(BF16)
    pkv = _dot(hb, w_ref[:, c0:c0 + 2 * ATT_KV_WIDTH])
    c0 += 2 * ATT_KV_WIDTH
    kv_ref[0, :, 0:ATT_KV_WIDTH] = _head_rms(pkv[:, 0:ATT_KV_WIDTH], kw_ref[...], ATT_HEAD_DIM).astype(BF16)
    kv_ref[0, :, ATT_KV_WIDTH:] = pkv[:, ATT_KV_WIDTH:].astype(BF16)
    za_ref[0] = _silu(_dot(hb, w_ref[:, c0:c0 + ATT_WIDTH])).astype(BF16)


def _even_in_proj(x, mod, nw, w_in, qw2, kw2, tm):
    bsz, s, d = x.shape
    n = w_in.shape[1]
    row = lambda b, i: (b, i, 0)
    const2 = lambda b, i: (0, 0)
    widths = (POOL_WIDTH, POOL_WIDTH, ATT_WIDTH, 2 * ATT_KV_WIDTH, ATT_WIDTH)
    return pl.pallas_call(
        _even_in_kernel,
        grid=(bsz, s // tm),
        in_specs=[
            pl.BlockSpec((1, tm, d), row),
            pl.BlockSpec((1, 3, d), lambda b, i: (b, 0, 0)),
            pl.BlockSpec((1, d), const2),
            pl.BlockSpec((d, n), const2),
            pl.BlockSpec((1, LANES), const2),
            pl.BlockSpec((1, LANES), const2),
        ],
        out_specs=[pl.BlockSpec((1, tm, w), row) for w in widths],
        out_shape=[jax.ShapeDtypeStruct((bsz, s, w), BF16) for w in widths],
        compiler_params=pltpu.CompilerParams(
            dimension_semantics=("parallel", "parallel"), vmem_limit_bytes=VMEM_LIMIT),
        name="even_in_proj",
    )(x, mod, nw, w_in, qw2, kw2)


def _even_mix_kernel(sink_ref, x_ref, mod_ref, u_ref, up_ref, un_ref, zp_ref, q_ref,
                     kv_ref, kvp_ref, kvn_ref, za_ref, wp_ref, ps_ref, bias_ref, wo_ref,
                     o_ref, ue_scr, kve_scr, mix_scr, *, seq_len):
    tq = x_ref.shape[1]
    i = pl.program_id(1)
    last = pl.num_programs(1) - 1

    ue_scr[POOL_HALO:POOL_HALO + tq, :] = u_ref[0].astype(F32)
    ue_scr[0:POOL_HALO, :] = jnp.where(i > 0, up_ref[0].astype(F32), 0.0)
    ue_scr[POOL_HALO + tq:, :] = jnp.where(i < last, un_ref[0].astype(F32), 0.0)
    t = i * tq + lax.broadcasted_iota(jnp.int32, (tq, 1), 0)
    for g, w in enumerate(POOL_WINDOWS):
        cols = slice(g * POOL_GROUP_DIM, (g + 1) * POOL_GROUP_DIM)
        tot = None
        for dlt in range(-(w // 2), w // 2):
            part = ue_scr[POOL_HALO + dlt:POOL_HALO + dlt + tq, cols]
            tot = part if tot is None else tot + part
        lo = jnp.clip(t - w // 2, 0, seq_len - 1)
        hi = jnp.clip(t + w // 2 - 1, 0, seq_len - 1)
        cnt = (hi - lo + 1).astype(F32)
        pooled = tot / cnt - ue_scr[POOL_HALO:POOL_HALO + tq, cols]
        y = _dot(pooled.astype(BF16), wp_ref[g]) * ps_ref[:, cols]
        mix_scr[:, cols] = (y * zp_ref[0, :, cols].astype(F32)).astype(BF16)

    kve_scr[0:BLOCK, :] = kvp_ref[0]
    kve_scr[BLOCK:BLOCK + tq, :] = kv_ref[0]
    kve_scr[BLOCK + tq:, :] = kvn_ref[0]
    col = lax.broadcasted_iota(jnp.int32, (1, 3 * BLOCK), 1)
    nblk = tq // BLOCK
    for j in range(nblk):
        rows = slice(j * BLOCK, (j + 1) * BLOCK)
        krows = slice(j * BLOCK, (j + 3) * BLOCK)
        edge = None
        if j == 0:
            edge = jnp.where((col < BLOCK) & (i == 0), NEG, 0.0)
        if j == nblk - 1:
            e2 = jnp.where((col >= 2 * BLOCK) & (i == last), NEG, 0.0)
            edge = e2 if edge is None else edge + e2
        for h in range(ATT_HEADS):
            kvh = h // ATT_GROUP
            qh = q_ref[0, rows, h * ATT_HEAD_DIM:(h + 1) * ATT_HEAD_DIM]
            kh = kve_scr[krows, kvh * ATT_HEAD_DIM:(kvh + 1) * ATT_HEAD_DIM]
            vh = kve_scr[krows, ATT_KV_WIDTH + kvh * ATT_HEAD_DIM:ATT_KV_WIDTH + (kvh + 1) * ATT_HEAD_DIM]
            s = _dot_nt(qh, kh) + bias_ref[h]
            if edge is not None:
                s = s + edge
            sink = sink_ref[h]
            m = jnp.maximum(jnp.max(s, axis=-1, keepdims=True), sink)
            p = jnp.exp(s - m)
            den = jnp.sum(p, axis=-1, keepdims=True) + jnp.exp(sink - m)
            o = _dot(p.astype(BF16), vh) / den
            za = za_ref[0, rows, h * ATT_HEAD_DIM:(h + 1) * ATT_HEAD_DIM].astype(F32)
            mix_scr[rows, POOL_WIDTH + h * ATT_HEAD_DIM:POOL_WIDTH + (h + 1) * ATT_HEAD_DIM] = (
                (o * za).astype(BF16))

    y = _dot(mix_scr[...], wo_ref[...])
    gate = mod_ref[0][2:3, :]
    o_ref[0] = x_ref[0] + gate * y


def _even_mix(x, mod, u, zp, q, kv, za, w_pool, pool_scale, sink, bias, w_out, tq):
    bsz, s, d = x.shape
    nt = s // tq
    row = lambda b, i: (b, i, 0)
    const2 = lambda b, i: (0, 0)
    const3 = lambda b, i: (0, 0, 0)
    hp = tq // POOL_HALO
    hb = tq // BLOCK
    kernel = functools.partial(_even_mix_kernel, seq_len=s)
    return pl.pallas_call(
        kernel,
        grid=(bsz, nt),
        in_specs=[
            pl.BlockSpec(memory_space=pltpu.SMEM),
            pl.BlockSpec((1, tq, d), row),
            pl.BlockSpec((1, 3, d), lambda b, i: (b, 0, 0)),
            pl.BlockSpec((1, tq, POOL_WIDTH), row),
            pl.BlockSpec((1, POOL_HALO, POOL_WIDTH), lambda b, i: (b, jnp.maximum(i * hp - 1, 0), 0)),
            pl.BlockSpec((1, POOL_HALO, POOL_WIDTH),
                         lambda b, i: (b, jnp.minimum((i + 1) * hp, s // POOL_HALO - 1), 0)),
            pl.BlockSpec((1, tq, POOL_WIDTH), row),
            pl.BlockSpec((1, tq, ATT_WIDTH), row),
            pl.BlockSpec((1, tq, 2 * ATT_KV_WIDTH), row),
            pl.BlockSpec((1, BLOCK, 2 * ATT_KV_WIDTH), lambda b, i: (b, jnp.maximum(i * hb - 1, 0), 0)),
            pl.BlockSpec((1, BLOCK, 2 * ATT_KV_WIDTH),
                         lambda b, i: (b, jnp.minimum((i + 1) * hb, s // BLOCK - 1), 0)),
            pl.BlockSpec((1, tq, ATT_WIDTH), row),
            pl.BlockSpec(w_pool.shape, const3),
            pl.BlockSpec((1, POOL_WIDTH), const2),
            pl.BlockSpec(bias.shape, const3),
            pl.BlockSpec(w_out.shape, const2),
        ],
        out_specs=pl.BlockSpec((1, tq, d), row),
        out_shape=jax.ShapeDtypeStruct((bsz, s, d), F32),
        scratch_shapes=[
            pltpu.VMEM((tq + 2 * POOL_HALO, POOL_WIDTH), F32),
            pltpu.VMEM((tq + 2 * BLOCK, 2 * ATT_KV_WIDTH), BF16),
            pltpu.VMEM((tq, d), BF16),
        ],
        compiler_params=pltpu.CompilerParams(
            dimension_semantics=("parallel", "parallel"), vmem_limit_bytes=VMEM_LIMIT),
        name="even_mix",
    )(sink, x, mod, u, u, u, zp, q, kv, kv, kv, za, w_pool, pool_scale, bias, w_out)


def _attn_bias():
    r = jnp.arange(BLOCK)[:, None]
    c = jnp.arange(3 * BLOCK)[None, :]
    dist = jnp.abs(r + BLOCK - c)
    slopes = 2.0 ** (-8.0 * jnp.arange(1, ATT_HEADS + 1, dtype=F32) / ATT_HEADS)
    bias = -slopes[:, None, None] * dist.astype(F32)[None]
    return jnp.where((dist <= WINDOW)[None], bias, NEG)


def _log_sigmoid(x):
    return jnp.minimum(x, 0.0) - jnp.log(1.0 + jnp.exp(-jnp.abs(x)))


def _odd_in_kernel(x_ref, mod_ref, nw_ref, w_ref, wa_ref, wg_ref, bg_ref,
                   q_ref, k_ref, v_ref, z_ref, gf_ref, gb_ref):
    hb = _norm_modulate(x_ref[0], nw_ref[...], mod_ref[0])
    c0 = 0
    q_ref[0] = (_dot(hb, w_ref[:, c0:c0 + GLA_KEY_WIDTH]) * (GLA_DK ** -0.5)).astype(BF16)
    c0 += GLA_KEY_WIDTH
    k_ref[0] = _dot(hb, w_ref[:, c0:c0 + GLA_KEY_WIDTH]).astype(BF16)
    c0 += GLA_KEY_WIDTH
    v_ref[0] = _dot(hb, w_ref[:, c0:c0 + GLA_VAL_WIDTH]).astype(BF16)
    c0 += GLA_VAL_WIDTH
    z_ref[0] = _silu(_dot(hb, w_ref[:, c0:c0 + GLA_VAL_WIDTH])).astype(BF16)
    a = _dot(hb, wa_ref[...]).astype(BF16)
    logits = _dot(a, wg_ref[...]) + bg_ref[...]
    g = _log_sigmoid(logits) * (1.0 / GLA_GATE_NORMALIZER)
    gf_ref[0] = g[:, 0:GLA_KEY_WIDTH]
    gb_ref[0] = g[:, GLA_KEY_WIDTH:]


def _odd_in_proj(x, mod, nw, w_main, w_a, w_g, b_g, tm):
    bsz, s, d = x.shape
    row = lambda b, i: (b, i, 0)
    const2 = lambda b, i: (0, 0)
    outs = ((GLA_KEY_WIDTH, BF16), (GLA_KEY_WIDTH, BF16), (GLA_VAL_WIDTH, BF16), (GLA_VAL_WIDTH, BF16),
            (GLA_KEY_WIDTH, F32), (GLA_KEY_WIDTH, F32))
    return pl.pallas_call(
        _odd_in_kernel,
        grid=(bsz, s // tm),
        in_specs=[
            pl.BlockSpec((1, tm, d), row),
            pl.BlockSpec((1, 3, d), lambda b, i: (b, 0, 0)),
            pl.BlockSpec((1, d), const2),
            pl.BlockSpec(w_main.shape, const2),
            pl.BlockSpec(w_a.shape, const2),
            pl.BlockSpec(w_g.shape, const2),
            pl.BlockSpec(b_g.shape, const2),
        ],
        out_specs=[pl.BlockSpec((1, tm, w), row) for w, _ in outs],
        out_shape=[jax.ShapeDtypeStruct((bsz, s, w), dt) for w, dt in outs],
        compiler_params=pltpu.CompilerParams(
            dimension_semantics=("parallel", "parallel"), vmem_limit_bytes=VMEM_LIMIT),
        name="odd_in_proj",
    )(x, mod, nw, w_main, w_a, w_g, b_g)


def _cumsum_mats():
    r = lax.broadcasted_iota(jnp.int32, (GLA_CHUNK, GLA_CHUNK), 0)
    c = lax.broadcasted_iota(jnp.int32, (GLA_CHUNK, GLA_CHUNK), 1)
    return (c <= r).astype(BF16), (c >= r).astype(BF16), c <= r


def _chunk_scan(tri, g):
    hi, lo = _split_bf16(g)
    return _dot(tri, hi) + _dot(tri, lo)


def _gla_kernel(x_ref, mod_ref, q_ref, k_ref, v_ref, z_ref, gf_ref, gb_ref, gw_ref, wo_ref,
                o_ref, sf_scr, sb_scr, snap_scr, mix_scr, *, nt):
    tb = x_ref.shape[1]
    nch = tb // GLA_CHUNK
    j = pl.program_id(1)
    lower, upper, causal = _cumsum_mats()

    @pl.when(j == 0)
    def _():
        sf_scr[...] = jnp.zeros_like(sf_scr)
        sb_scr[...] = jnp.zeros_like(sb_scr)

    @pl.when(j < nt)
    def _():
        blk = nt - 1 - j
        for cc in range(nch - 1, -1, -1):
            rows = slice(cc * GLA_CHUNK, (cc + 1) * GLA_CHUNK)
            beta = _chunk_scan(upper, gb_ref[0, rows, :])
            for h in range(GLA_HEADS):
                kc = slice(h * GLA_DK, (h + 1) * GLA_DK)
                vc = slice(h * GLA_DV, (h + 1) * GLA_DV)
                bh = beta[:, kc]
                first = bh[0:1, :]
                kx = (k_ref[0, rows, kc].astype(F32) * jnp.exp(first - bh)).astype(BF16)
                st = sb_scr[h]
                snap_scr[blk * nch + cc, h] = st.astype(BF16)
                sb_scr[h] = jnp.exp(first) * st + _dot_tn(v_ref[0, rows, vc], kx)

    @pl.when(j >= nt)
    def _():
        blk = j - nt
        for cc in range(nch):
            rows = slice(cc * GLA_CHUNK, (cc + 1) * GLA_CHUNK)
            bfw = _chunk_scan(lower, gf_ref[0, rows, :])
            beta = _chunk_scan(upper, gb_ref[0, rows, :])
            for h in range(GLA_HEADS):
                kc = slice(h * GLA_DK, (h + 1) * GLA_DK)
                vc = slice(h * GLA_DV, (h + 1) * GLA_DV)
                qh = q_ref[0, rows, kc].astype(F32)
                kh = k_ref[0, rows, kc].astype(F32)
                vh = v_ref[0, rows, vc]
                bf = bfw[:, kc]
                bb = beta[:, kc]
                rf = bf[GLA_CHUNK // 2:GLA_CHUNK // 2 + 1, :]
                rb = bb[GLA_CHUNK // 2 - 1:GLA_CHUNK // 2, :]
                a_f = _dot_nt((qh * jnp.exp(bf - rf)).astype(BF16), (kh * jnp.exp(rf - bf)).astype(BF16))
                a_b = _dot_nt((qh * jnp.exp(bb - rb)).astype(BF16), (kh * jnp.exp(rb - bb)).astype(BF16))
                att = jnp.where(causal, a_f, a_b).astype(BF16)
                sf = sf_scr[h]
                o = _dot(att, vh)
                o = o + _dot_nt((qh * jnp.exp(bf)).astype(BF16), sf.astype(BF16))
                o = o + _dot_nt((qh * jnp.exp(bb)).astype(BF16), snap_scr[blk * nch + cc, h])
                lastf = bf[GLA_CHUNK - 1:GLA_CHUNK, :]
                kx = (kh * jnp.exp(lastf - bf)).astype(BF16)
                sf_scr[h] = jnp.exp(lastf) * sf + _dot_tn(vh, kx)
                ms = jnp.mean(o * o, axis=-1, keepdims=True)
                on = (o * lax.rsqrt(ms + EPS)) * gw_ref[...]
                mix_scr[rows, vc] = (on * z_ref[0, rows, vc].astype(F32)).astype(BF16)
        y = _dot(mix_scr[...], wo_ref[...])
        gate = mod_ref[0][2:3, :]
        o_ref[0] = x_ref[0] + gate * y


def _gla_mix(x, mod, q, k, v, z, gf, gb, gw, w_out, tb):
    bsz, s, d = x.shape
    nt = s // tb
    const2 = lambda b, j: (0, 0)
    both = lambda b, j: (b, jnp.where(j < nt, nt - 1 - j, j - nt), 0)
    fwd_only = lambda b, j: (b, jnp.maximum(j - nt, 0), 0)
    return pl.pallas_call(
        functools.partial(_gla_kernel, nt=nt),
        grid=(bsz, 2 * nt),
        in_specs=[
            pl.BlockSpec((1, tb, d), fwd_only),
            pl.BlockSpec((1, 3, d), lambda b, j: (b, 0, 0)),
            pl.BlockSpec((1, tb, GLA_KEY_WIDTH), fwd_only),
            pl.BlockSpec((1, tb, GLA_KEY_WIDTH), both),
            pl.BlockSpec((1, tb, GLA_VAL_WIDTH), both),
            pl.BlockSpec((1, tb, GLA_VAL_WIDTH), fwd_only),
            pl.BlockSpec((1, tb, GLA_KEY_WIDTH), fwd_only),
            pl.BlockSpec((1, tb, GLA_KEY_WIDTH), both),
            pl.BlockSpec((1, GLA_DV), const2),
            pl.BlockSpec(w_out.shape, const2),
        ],
        out_specs=pl.BlockSpec((1, tb, d), fwd_only),
        out_shape=jax.ShapeDtypeStruct((bsz, s, d), F32),
        scratch_shapes=[
            pltpu.VMEM((GLA_HEADS, GLA_DV, GLA_DK), F32),
            pltpu.VMEM((GLA_HEADS, GLA_DV, GLA_DK), F32),
            pltpu.VMEM((s // GLA_CHUNK, GLA_HEADS, GLA_DV, GLA_DK), BF16),
            pltpu.VMEM((tb, d), BF16),
        ],
        compiler_params=pltpu.CompilerParams(
            dimension_semantics=("parallel", "arbitrary"), vmem_limit_bytes=VMEM_LIMIT),
        name="gla_mix",
    )(x, mod, q, k, v, z, gf, gb, gw, w_out)


def kernel(x, c, norm_w, w_ada, b_ada, w_in_a, w_pool, pool_scale, q_norm_w, k_norm_w, attn_sink,
           w_out_a, w_in_c, w_gate_up, b_gate, gla_norm_w, w_out_c):
    bsz, s, d = x.shape
    assert d == D_MODEL and s % 512 == 0
    mod_all = _modulation(c, w_ada, b_ada).reshape(DEPTH, bsz, 3, d)
    bias = _attn_bias()
    zeros_g = jnp.zeros((GLA_GATE_RANK, GLA_KEY_WIDTH), F32)
    for l in range(DEPTH):
        mod = mod_all[l]
        nw = norm_w[l].reshape(1, d)
        if l % 2 == 0:
            i = l // 2
            qw2 = jnp.tile(q_norm_w[i], 2).reshape(1, LANES)
            kw2 = jnp.tile(k_norm_w[i], 2).reshape(1, LANES)
            u, zp, q, kv, za = _even_in_proj(x, mod, nw, w_in_a[i].astype(BF16), qw2, kw2, tm=512)
            x = _even_mix(x, mod, u, zp, q, kv, za, w_pool[i].astype(BF16),
                          pool_scale[i].reshape(1, POOL_WIDTH), attn_sink[i], bias,
                          w_out_a[i].astype(BF16), tq=512)
        else:
            jdx = l // 2
            n_main = 2 * GLA_KEY_WIDTH + 2 * GLA_VAL_WIDTH
            w_main = w_in_c[jdx][:, :n_main].astype(BF16)
            w_a = w_in_c[jdx][:, n_main:].astype(BF16)
            w_g = jnp.concatenate([
                jnp.concatenate([w_gate_up[jdx, 0], zeros_g], axis=1),
                jnp.concatenate([zeros_g, w_gate_up[jdx, 1]], axis=1)], axis=0).astype(BF16)
            b_g = b_gate[jdx].reshape(1, 2 * GLA_KEY_WIDTH)
            q, k, v, z, gf, gb = _odd_in_proj(x, mod, nw, w_main, w_a, w_g, b_g, tm=512)
            x = _gla_mix(x, mod, q, k, v, z, gf, gb, gla_norm_w[jdx].reshape(1, GLA_DV),
                         w_out_c[jdx].astype(BF16), tb=512)
    return x
```

```python
import functools

import jax
import jax.numpy as jnp
from jax import lax
from jax.experimental import pallas as pl
from jax.experimental.pallas import tpu as pltpu

F32 = jnp.float32
BF16 = jnp.bfloat16

D_MODEL = 1024
DEPTH = 4
POOL_WINDOWS = (2, 4, 8, 16)
POOL_WIDTH = 512
POOL_GROUP_DIM = 128
ATT_HEADS = 8
ATT_KV_HEADS = 2
ATT_GROUP = ATT_HEADS // ATT_KV_HEADS
ATT_HEAD_DIM = 64
ATT_WIDTH = 512
ATT_KV_WIDTH = 128
WINDOW = 128
BLOCK = 128
GLA_HEADS = 4
GLA_KEY_WIDTH = 512
GLA_VAL_WIDTH = 1024
GLA_DK = 128
GLA_DV = 256
GLA_GATE_RANK = 16
GLA_GATE_NORMALIZER = 16.0
EPS = 1e-6
NEG = -1e30
LOG2E = 1.4426950408889634

LANES = 128
POOL_HALO = 64
GLA_CHUNK = 128
SUB_ROWS = 512
VMEM_LIMIT = 56 * 1024 * 1024


def _dot(a, b):
    return jnp.dot(a, b, preferred_element_type=F32)


def _dot_nt(a, b):
    return lax.dot_general(a, b, (((1,), (1,)), ((), ())), preferred_element_type=F32)


def _dot_tn(a, b):
    return lax.dot_general(a, b, (((0,), (0,)), ((), ())), preferred_element_type=F32)


def _split_bf16(x):
    hi = x.astype(BF16)
    lo = (x - hi.astype(F32)).astype(BF16)
    return hi, lo


def _silu(x):
    return x * jax.nn.sigmoid(x)


def _mod_kernel(c_ref, w_ref, b_ref, o_ref):
    sc = _silu(c_ref[...])
    sc_hi, sc_lo = _split_bf16(sc)
    w_hi, w_lo = _split_bf16(w_ref[0])
    acc = _dot(sc_hi, w_hi) + _dot(sc_lo, w_hi) + _dot(sc_hi, w_lo)
    o_ref[0] = acc + b_ref[0]


def _modulation(c, w_ada, b_ada):
    depth, d, n = w_ada.shape
    bsz = c.shape[0]
    tn = 1024
    return pl.pallas_call(
        _mod_kernel,
        grid=(depth, n // tn),
        in_specs=[
            pl.BlockSpec((bsz, d), lambda l, j: (0, 0)),
            pl.BlockSpec((1, d, tn), lambda l, j: (l, 0, j)),
            pl.BlockSpec((1, 1, tn), lambda l, j: (l, 0, j)),
        ],
        out_specs=pl.BlockSpec((1, bsz, tn), lambda l, j: (l, 0, j)),
        out_shape=jax.ShapeDtypeStruct((depth, bsz, n), F32),
        compiler_params=pltpu.CompilerParams(
            dimension_semantics=("parallel", "parallel"), vmem_limit_bytes=VMEM_LIMIT),
        name="adaln_mod",
    )(c, w_ada, b_ada.reshape(depth, 1, n))


def _pool_fold_kernel(w_ref, wp_ref, o_ref):
    w_hi, w_lo = _split_bf16(w_ref[...])
    p_hi, p_lo = _split_bf16(wp_ref[0])
    o_ref[...] = (_dot(w_hi, p_hi) + _dot(w_lo, p_hi) + _dot(w_hi, p_lo)).astype(BF16)


def _pool_fold(w_in, w_pool):
    d = w_in.shape[0]
    groups, gd, _ = w_pool.shape
    return pl.pallas_call(
        _pool_fold_kernel,
        grid=(groups,),
        in_specs=[
            pl.BlockSpec((d, gd), lambda g: (0, g)),
            pl.BlockSpec((1, gd, gd), lambda g: (g, 0, 0)),
        ],
        out_specs=pl.BlockSpec((d, gd), lambda g: (0, g)),
        out_shape=jax.ShapeDtypeStruct((d, groups * gd), BF16),
        compiler_params=pltpu.CompilerParams(dimension_semantics=("parallel",)),
        name="pool_fold",
    )(w_in, w_pool)


def _norm_modulate(x, nw, mod):
    ms = jnp.mean(x * x, axis=-1, keepdims=True)
    y = (x * lax.rsqrt(ms + EPS)) * nw
    shift = mod[0:1, :]
    scale = mod[1:2, :]
    return (y * (1.0 + scale) + shift).astype(BF16)


def _head_rms(p, w2, head_dim):
    assert 2 * head_dim == LANES and p.shape[-1] == LANES
    lane = lax.broadcasted_iota(jnp.int32, (1, LANES), 1)
    first = lane < head_dim
    sq = p * p
    s0 = jnp.sum(jnp.where(first, sq, 0.0), axis=-1, keepdims=True)
    s1 = jnp.sum(jnp.where(first, 0.0, sq), axis=-1, keepdims=True)
    ms = jnp.where(first, s0, s1) * (1.0 / head_dim)
    return (p * lax.rsqrt(ms + EPS)) * w2


def _even_in_kernel(x_ref, mod_ref, nw_ref, w_ref, qwt_ref, kw_ref,
                    u_ref, zp_ref, qt_ref, k_ref, vt_ref, za_ref):
    for r in range(x_ref.shape[1] // SUB_ROWS):
        tok = slice(r * SUB_ROWS, (r + 1) * SUB_ROWS)
        hb = _norm_modulate(x_ref[0, tok, :], nw_ref[...], mod_ref[0])
        c0 = 0
        u_ref[0, tok, :] = _dot(hb, w_ref[:, c0:c0 + POOL_WIDTH]).astype(BF16)
        c0 += POOL_WIDTH
        zp_ref[0, tok, :] = _silu(_dot(hb, w_ref[:, c0:c0 + POOL_WIDTH])).astype(BF16)
        c0 += POOL_WIDTH
        pqt = _dot(hb, w_ref[:, c0:c0 + ATT_WIDTH]).T
        c0 += ATT_WIDTH
        for h in range(ATT_HEADS):
            rows = slice(h * ATT_HEAD_DIM, (h + 1) * ATT_HEAD_DIM)
            ph = pqt[rows, :]
            ms = jnp.mean(ph * ph, axis=0, keepdims=True)
            qt_ref[0, rows, tok] = ((ph * lax.rsqrt(ms + EPS)) * qwt_ref[rows, :]).astype(BF16)
        pkv = _dot(hb, w_ref[:, c0:c0 + 2 * ATT_KV_WIDTH])
        c0 += 2 * ATT_KV_WIDTH
        k_ref[0, tok, :] = _head_rms(pkv[:, 0:ATT_KV_WIDTH], kw_ref[...], ATT_HEAD_DIM).astype(BF16)
        vt_ref[0, :, tok] = pkv[:, ATT_KV_WIDTH:].T.astype(BF16)
        za_ref[0, tok, :] = _silu(_dot(hb, w_ref[:, c0:c0 + ATT_WIDTH])).astype(BF16)


def _even_in_proj(x, mod, nw, w_in, qwt, kw2, tm):
    bsz, s, d = x.shape
    n = w_in.shape[1]
    row = lambda b, i: (b, i, 0)
    col = lambda b, i: (b, 0, i)
    const2 = lambda b, i: (0, 0)
    outs = (((s, POOL_WIDTH), (tm, POOL_WIDTH), row), ((s, POOL_WIDTH), (tm, POOL_WIDTH), row),
            ((ATT_WIDTH, s), (ATT_WIDTH, tm), col), ((s, ATT_KV_WIDTH), (tm, ATT_KV_WIDTH), row),
            ((ATT_KV_WIDTH, s), (ATT_KV_WIDTH, tm), col), ((s, ATT_WIDTH), (tm, ATT_WIDTH), row))
    return pl.pallas_call(
        _even_in_kernel,
        grid=(bsz, s // tm),
        in_specs=[
            pl.BlockSpec((1, tm, d), row),
            pl.BlockSpec((1, 3, d), lambda b, i: (b, 0, 0)),
            pl.BlockSpec((1, d), const2),
            pl.BlockSpec((d, n), const2),
            pl.BlockSpec((ATT_WIDTH, SUB_ROWS), const2),
            pl.BlockSpec((1, LANES), const2),
        ],
        out_specs=[pl.BlockSpec((1,) + blk, imap) for _, blk, imap in outs],
        out_shape=[jax.ShapeDtypeStruct((bsz,) + full, BF16) for full, _, _ in outs],
        compiler_params=pltpu.CompilerParams(
            dimension_semantics=("parallel", "parallel"), vmem_limit_bytes=VMEM_LIMIT),
        name="even_in_proj",
    )(x, mod, nw, w_in, qwt, kw2)


def _even_mix_kernel(x_ref, mod_ref, u_ref, up_ref, un_ref, zp_ref, qt_ref, k_ref, kp_ref, kn_ref,
                     vt_ref, vtp_ref, vtn_ref, za_ref, band_ref, icnt_ref, ps_ref, bias_ref, sink_ref,
                     wo_ref, o_ref, ue_scr, ke_scr, vte_scr, ot_scr, mix_scr):
    tq = x_ref.shape[1]
    i = pl.program_id(1)
    last = pl.num_programs(1) - 1
    nblk = tq // BLOCK
    kspan = 3 * BLOCK

    ue_scr[0:POOL_HALO, :] = jnp.where(i > 0, up_ref[0], jnp.zeros_like(up_ref[0]))
    ue_scr[POOL_HALO:POOL_HALO + tq, :] = u_ref[0]
    ue_scr[POOL_HALO + tq:, :] = jnp.where(i < last, un_ref[0], jnp.zeros_like(un_ref[0]))
    tots = {}
    for j in range(nblk):
        for g in range(len(POOL_WINDOWS)):
            cols = slice(g * POOL_GROUP_DIM, (g + 1) * POOL_GROUP_DIM)
            tots[j, g] = _dot(band_ref[g], ue_scr[j * BLOCK:j * BLOCK + BLOCK + 2 * POOL_HALO, cols])
    for j in range(nblk):
        rows = slice(j * BLOCK, (j + 1) * BLOCK)
        for g, w in enumerate(POOL_WINDOWS):
            cols = slice(g * POOL_GROUP_DIM, (g + 1) * POOL_GROUP_DIM)
            inv_cnt = 1.0 / w
            if j == 0:
                inv_cnt = jnp.where(i == 0, icnt_ref[0, :, cols], inv_cnt)
            if j == nblk - 1:
                inv_cnt = jnp.where(i == last, icnt_ref[1, :, cols], inv_cnt)
            y = (tots[j, g] * inv_cnt - u_ref[0, rows, cols].astype(F32)) * ps_ref[:, cols]
            mix_scr[rows, cols] = (y * zp_ref[0, rows, cols].astype(F32)).astype(BF16)

    ke_scr[0:BLOCK, :] = kp_ref[0]
    ke_scr[BLOCK:BLOCK + tq, :] = k_ref[0]
    ke_scr[BLOCK + tq:, :] = kn_ref[0]
    vte_scr[:, 0:BLOCK] = vtp_ref[0]
    vte_scr[:, BLOCK:BLOCK + tq] = vt_ref[0]
    vte_scr[:, BLOCK + tq:] = vtn_ref[0]
    krow = lax.broadcasted_iota(jnp.int32, (kspan, 1), 0)
    units = [(j, kh) for j in range(nblk) for kh in range(ATT_KV_HEADS)]

    def scores(j, kh):
        rows = slice(j * BLOCK, (j + 1) * BLOCK)
        hd = slice(kh * ATT_HEAD_DIM, (kh + 1) * ATT_HEAD_DIM)
        qs = jnp.concatenate(
            [qt_ref[0, (kh * ATT_GROUP + g) * ATT_HEAD_DIM:(kh * ATT_GROUP + g + 1) * ATT_HEAD_DIM, rows]
             for g in range(ATT_GROUP)], axis=1)
        return _dot(ke_scr[j * BLOCK:j * BLOCK + kspan, hd], qs)

    st_next = scores(*units[0])
    for n, (j, kh) in enumerate(units):
        st = st_next + bias_ref[kh]
        if n + 1 < len(units):
            st_next = scores(*units[n + 1])
        if j == 0:
            st = st + jnp.where((krow < BLOCK) & (i == 0), NEG, 0.0)
        if j == nblk - 1:
            st = st + jnp.where((krow >= 2 * BLOCK) & (i == last), NEG, 0.0)
        hd = slice(kh * ATT_HEAD_DIM, (kh + 1) * ATT_HEAD_DIM)
        sink = sink_ref[kh:kh + 1, :]
        m = jnp.maximum(jnp.max(st, axis=0, keepdims=True), sink)
        p = jnp.exp2(st - m)
        den = jnp.sum(p, axis=0, keepdims=True) + jnp.exp2(sink - m)
        ot = _dot(vte_scr[hd, j * BLOCK:j * BLOCK + kspan], p.astype(BF16)) * (1.0 / den)
        for g in range(ATT_GROUP):
            h = kh * ATT_GROUP + g
            ot_scr[j, h * ATT_HEAD_DIM:(h + 1) * ATT_HEAD_DIM, :] = ot[:, g * BLOCK:(g + 1) * BLOCK]
        if kh == ATT_KV_HEADS - 1:
            rows = slice(j * BLOCK, (j + 1) * BLOCK)
            o = ot_scr[j].T
            mix_scr[rows, POOL_WIDTH:] = (o * za_ref[0, rows, :].astype(F32)).astype(BF16)

    y = _dot(mix_scr[...], wo_ref[...])
    gate = mod_ref[0][2:3, :]
    o_ref[0] = x_ref[0] + gate * y


def _even_mix(x, mod, u, zp, qt, k, vt, za, band, icnt, pool_scale, bias, sink, w_out, tq):
    bsz, s, d = x.shape
    nt = s // tq
    row = lambda b, i: (b, i, 0)
    col = lambda b, i: (b, 0, i)
    const2 = lambda b, i: (0, 0)
    const3 = lambda b, i: (0, 0, 0)
    hp = tq // POOL_HALO
    hb = tq // BLOCK
    prev_row = lambda n: (lambda b, i: (b, jnp.maximum(i * n - 1, 0), 0))
    next_row = lambda n, tot: (lambda b, i: (b, jnp.minimum((i + 1) * n, tot - 1), 0))
    return pl.pallas_call(
        _even_mix_kernel,
        grid=(bsz, nt),
        in_specs=[
            pl.BlockSpec((1, tq, d), row),
            pl.BlockSpec((1, 3, d), lambda b, i: (b, 0, 0)),
            pl.BlockSpec((1, tq, POOL_WIDTH), row),
            pl.BlockSpec((1, POOL_HALO, POOL_WIDTH), prev_row(hp)),
            pl.BlockSpec((1, POOL_HALO, POOL_WIDTH), next_row(hp, s // POOL_HALO)),
            pl.BlockSpec((1, tq, POOL_WIDTH), row),
            pl.BlockSpec((1, ATT_WIDTH, tq), col),
            pl.BlockSpec((1, tq, ATT_KV_WIDTH), row),
            pl.BlockSpec((1, BLOCK, ATT_KV_WIDTH), prev_row(hb)),
            pl.BlockSpec((1, BLOCK, ATT_KV_WIDTH), next_row(hb, s // BLOCK)),
            pl.BlockSpec((1, ATT_KV_WIDTH, tq), col),
            pl.BlockSpec((1, ATT_KV_WIDTH, BLOCK), lambda b, i: (b, 0, jnp.maximum(i * hb - 1, 0))),
            pl.BlockSpec((1, ATT_KV_WIDTH, BLOCK), lambda b, i: (b, 0, jnp.minimum((i + 1) * hb, s // BLOCK - 1))),
            pl.BlockSpec((1, tq, ATT_WIDTH), row),
            pl.BlockSpec(band.shape, const3),
            pl.BlockSpec(icnt.shape, const3),
            pl.BlockSpec((1, POOL_WIDTH), const2),
            pl.BlockSpec(bias.shape, const3),
            pl.BlockSpec(sink.shape, const2),
            pl.BlockSpec(w_out.shape, const2),
        ],
        out_specs=pl.BlockSpec((1, tq, d), row),
        out_shape=jax.ShapeDtypeStruct((bsz, s, d), F32),
        scratch_shapes=[
            pltpu.VMEM((tq + 2 * POOL_HALO, POOL_WIDTH), BF16),
            pltpu.VMEM((tq + 2 * BLOCK, ATT_KV_WIDTH), BF16),
            pltpu.VMEM((ATT_KV_WIDTH, tq + 2 * BLOCK), BF16),
            pltpu.VMEM((tq // BLOCK, ATT_WIDTH, BLOCK), F32),
            pltpu.VMEM((tq, d), BF16),
        ],
        compiler_params=pltpu.CompilerParams(
            dimension_semantics=("parallel", "parallel"), vmem_limit_bytes=VMEM_LIMIT),
        name="even_mix",
    )(x, mod, u, u, u, zp, qt, k, k, k, vt, vt, vt, za, band, icnt, pool_scale, bias, sink, w_out)


def _attn_bias():
    r = jnp.arange(BLOCK)[:, None]
    c = jnp.arange(3 * BLOCK)[None, :]
    dist = jnp.abs(r + BLOCK - c)
    slopes = 2.0 ** (-8.0 * jnp.arange(1, ATT_HEADS + 1, dtype=F32) / ATT_HEADS)
    bias = -(LOG2E * slopes)[:, None, None] * dist.astype(F32)[None]
    bias = jnp.where((dist <= WINDOW)[None], bias, NEG)
    bias = bias.reshape(ATT_KV_HEADS, ATT_GROUP, BLOCK, 3 * BLOCK)
    return jnp.transpose(bias, (0, 3, 1, 2)).reshape(ATT_KV_HEADS, 3 * BLOCK, ATT_GROUP * BLOCK)


def _pool_band():
    r = jnp.arange(BLOCK)[:, None]
    c = jnp.arange(BLOCK + 2 * POOL_HALO)[None, :]
    off = c - POOL_HALO - r
    return jnp.stack([((off >= -(w // 2)) & (off <= w // 2 - 1)) for w in POOL_WINDOWS]).astype(BF16)


def _pool_inv_count(seq_len):
    t = jnp.concatenate([jnp.arange(BLOCK), jnp.arange(seq_len - BLOCK, seq_len)])[:, None]
    w = jnp.repeat(jnp.asarray(POOL_WINDOWS), POOL_GROUP_DIM)[None, :]
    lo = jnp.clip(t - w // 2, 0, seq_len - 1)
    hi = jnp.clip(t + w // 2 - 1, 0, seq_len - 1)
    return (1.0 / (hi - lo + 1).astype(F32)).reshape(2, BLOCK, POOL_WIDTH)


def _log_sigmoid(x):
    return jnp.minimum(x, 0.0) - jnp.log(1.0 + jnp.exp(-jnp.abs(x)))


def _odd_in_kernel(x_ref, mod_ref, nw_ref, w_ref, wa_ref, wg_ref, bg_ref,
                   q_ref, k_ref, v_ref, z_ref, gf_ref, gb_ref):
    for r in range(x_ref.shape[1] // SUB_ROWS):
        tok = slice(r * SUB_ROWS, (r + 1) * SUB_ROWS)
        hb = _norm_modulate(x_ref[0, tok, :], nw_ref[...], mod_ref[0])
        c0 = 0
        q_ref[0, tok, :] = (_dot(hb, w_ref[:, c0:c0 + GLA_KEY_WIDTH]) * (GLA_DK ** -0.5)).astype(BF16)
        c0 += GLA_KEY_WIDTH
        k_ref[0, tok, :] = _dot(hb, w_ref[:, c0:c0 + GLA_KEY_WIDTH]).astype(BF16)
        c0 += GLA_KEY_WIDTH
        v_ref[0, tok, :] = _dot(hb, w_ref[:, c0:c0 + GLA_VAL_WIDTH]).astype(BF16)
        c0 += GLA_VAL_WIDTH
        z_ref[0, tok, :] = _silu(_dot(hb, w_ref[:, c0:c0 + GLA_VAL_WIDTH])).astype(BF16)
        a = _dot(hb, wa_ref[...]).astype(BF16)
        logits = _dot(a, wg_ref[...]) + bg_ref[...]
        g = _log_sigmoid(logits) * (1.0 / GLA_GATE_NORMALIZER)
        gf_ref[0, tok, :] = g[:, 0:GLA_KEY_WIDTH]
        gb_ref[0, tok, :] = g[:, GLA_KEY_WIDTH:]


def _odd_in_proj(x, mod, nw, w_main, w_a, w_g, b_g, tm):
    bsz, s, d = x.shape
    row = lambda b, i: (b, i, 0)
    const2 = lambda b, i: (0, 0)
    outs = ((GLA_KEY_WIDTH, BF16), (GLA_KEY_WIDTH, BF16), (GLA_VAL_WIDTH, BF16), (GLA_VAL_WIDTH, BF16),
            (GLA_KEY_WIDTH, F32), (GLA_KEY_WIDTH, F32))
    return pl.pallas_call(
        _odd_in_kernel,
        grid=(bsz, s // tm),
        in_specs=[
            pl.BlockSpec((1, tm, d), row),
            pl.BlockSpec((1, 3, d), lambda b, i: (b, 0, 0)),
            pl.BlockSpec((1, d), const2),
            pl.BlockSpec(w_main.shape, const2),
            pl.BlockSpec(w_a.shape, const2),
            pl.BlockSpec(w_g.shape, const2),
            pl.BlockSpec(b_g.shape, const2),
        ],
        out_specs=[pl.BlockSpec((1, tm, w), row) for w, _ in outs],
        out_shape=[jax.ShapeDtypeStruct((bsz, s, w), dt) for w, dt in outs],
        compiler_params=pltpu.CompilerParams(
            dimension_semantics=("parallel", "parallel"), vmem_limit_bytes=VMEM_LIMIT),
        name="odd_in_proj",
    )(x, mod, nw, w_main, w_a, w_g, b_g)


def _cumsum_mats():
    r = lax.broadcasted_iota(jnp.int32, (GLA_CHUNK, GLA_CHUNK), 0)
    c = lax.broadcasted_iota(jnp.int32, (GLA_CHUNK, GLA_CHUNK), 1)
    return (c <= r).astype(BF16), (c >= r).astype(BF16), c <= r


def _chunk_scan(tri, g):
    hi, lo = _split_bf16(g)
    return _dot(tri, hi) + _dot(tri, lo)


def _gla_kernel(x_ref, mod_ref, q_ref, k_ref, v_ref, z_ref, gf_ref, gb_ref, gw_ref, wo_ref,
                o_ref, sf_scr, sb_scr, snap_scr, mix_scr, *, nt):
    tb = x_ref.shape[1]
    nch = tb // GLA_CHUNK
    j = pl.program_id(1)
    lower, upper, causal = _cumsum_mats()

    @pl.when(j == 0)
    def _():
        sf_scr[...] = jnp.zeros_like(sf_scr)
        sb_scr[...] = jnp.zeros_like(sb_scr)

    @pl.when(j < nt)
    def _():
        blk = nt - 1 - j
        for cc in range(nch - 1, -1, -1):
            rows = slice(cc * GLA_CHUNK, (cc + 1) * GLA_CHUNK)
            beta = _chunk_scan(upper, gb_ref[0, rows, :])
            for h in range(GLA_HEADS):
                kc = slice(h * GLA_DK, (h + 1) * GLA_DK)
                vc = slice(h * GLA_DV, (h + 1) * GLA_DV)
                bh = beta[:, kc]
                first = bh[0:1, :]
                kx = (k_ref[0, rows, kc].astype(F32) * jnp.exp(first - bh)).astype(BF16)
                st = sb_scr[h]
                snap_scr[blk * nch + cc, h] = st.astype(BF16)
                sb_scr[h] = jnp.exp(first) * st + _dot_tn(v_ref[0, rows, vc], kx)

    @pl.when(j >= nt)
    def _():
        blk = j - nt
        for cc in range(nch):
            rows = slice(cc * GLA_CHUNK, (cc + 1) * GLA_CHUNK)
            bfw = _chunk_scan(lower, gf_ref[0, rows, :])
            beta = _chunk_scan(upper, gb_ref[0, rows, :])
            for h in range(GLA_HEADS):
                kc = slice(h * GLA_DK, (h + 1) * GLA_DK)
                vc = slice(h * GLA_DV, (h + 1) * GLA_DV)
                qh = q_ref[0, rows, kc].astype(F32)
                kh = k_ref[0, rows, kc].astype(F32)
                vh = v_ref[0, rows, vc]
                bf = bfw[:, kc]
                bb = beta[:, kc]
                rf = bf[GLA_CHUNK // 2:GLA_CHUNK // 2 + 1, :]
                rb = bb[GLA_CHUNK // 2 - 1:GLA_CHUNK // 2, :]
                a_f = _dot_nt((qh * jnp.exp(bf - rf)).astype(BF16), (kh * jnp.exp(rf - bf)).astype(BF16))
                a_b = _dot_nt((qh * jnp.exp(bb - rb)).astype(BF16), (kh * jnp.exp(rb - bb)).astype(BF16))
                att = jnp.where(causal, a_f, a_b).astype(BF16)
                sf = sf_scr[h]
                o = _dot(att, vh)
                o = o + _dot_nt((qh * jnp.exp(bf)).astype(BF16), sf.astype(BF16))
                o = o + _dot_nt((qh * jnp.exp(bb)).astype(BF16), snap_scr[blk * nch + cc, h])
                lastf = bf[GLA_CHUNK - 1:GLA_CHUNK, :]
                kx = (kh * jnp.exp(lastf - bf)).astype(BF16)
                sf_scr[h] = jnp.exp(lastf) * sf + _dot_tn(vh, kx)
                ms = jnp.mean(o * o, axis=-1, keepdims=True)
                on = (o * lax.rsqrt(ms + EPS)) * gw_ref[...]
                mix_scr[rows, vc] = (on * z_ref[0, rows, vc].astype(F32)).astype(BF16)
        y = _dot(mix_scr[...], wo_ref[...])
        gate = mod_ref[0][2:3, :]
        o_ref[0] = x_ref[0] + gate * y


def _gla_mix(x, mod, q, k, v, z, gf, gb, gw, w_out, tb):
    bsz, s, d = x.shape
    nt = s // tb
    const2 = lambda b, j: (0, 0)
    both = lambda b, j: (b, jnp.where(j < nt, nt - 1 - j, j - nt), 0)
    fwd_only = lambda b, j: (b, jnp.maximum(j - nt, 0), 0)
    return pl.pallas_call(
        functools.partial(_gla_kernel, nt=nt),
        grid=(bsz, 2 * nt),
        in_specs=[
            pl.BlockSpec((1, tb, d), fwd_only),
            pl.BlockSpec((1, 3, d), lambda b, j: (b, 0, 0)),
            pl.BlockSpec((1, tb, GLA_KEY_WIDTH), fwd_only),
            pl.BlockSpec((1, tb, GLA_KEY_WIDTH), both),
            pl.BlockSpec((1, tb, GLA_VAL_WIDTH), both),
            pl.BlockSpec((1, tb, GLA_VAL_WIDTH), fwd_only),
            pl.BlockSpec((1, tb, GLA_KEY_WIDTH), fwd_only),
            pl.BlockSpec((1, tb, GLA_KEY_WIDTH), both),
            pl.BlockSpec((1, GLA_DV), const2),
            pl.BlockSpec(w_out.shape, const2),
        ],
        out_specs=pl.BlockSpec((1, tb, d), fwd_only),
        out_shape=jax.ShapeDtypeStruct((bsz, s, d), F32),
        scratch_shapes=[
            pltpu.VMEM((GLA_HEADS, GLA_DV, GLA_DK), F32),
            pltpu.VMEM((GLA_HEADS, GLA_DV, GLA_DK), F32),
            pltpu.VMEM((s // GLA_CHUNK, GLA_HEADS, GLA_DV, GLA_DK), BF16),
            pltpu.VMEM((tb, d), BF16),
        ],
        compiler_params=pltpu.CompilerParams(
            dimension_semantics=("parallel", "arbitrary"), vmem_limit_bytes=VMEM_LIMIT),
        name="gla_mix",
    )(x, mod, q, k, v, z, gf, gb, gw, w_out)


def kernel(x, c, norm_w, w_ada, b_ada, w_in_a, w_pool, pool_scale, q_norm_w, k_norm_w, attn_sink,
           w_out_a, w_in_c, w_gate_up, b_gate, gla_norm_w, w_out_c):
    bsz, s, d = x.shape
    assert d == D_MODEL and s % 512 == 0
    mod_all = _modulation(c, w_ada, b_ada).reshape(DEPTH, bsz, 3, d)
    bias = _attn_bias()
    band = _pool_band()
    icnt = _pool_inv_count(s)
    zeros_g = jnp.zeros((GLA_GATE_RANK, GLA_KEY_WIDTH), F32)
    for l in range(DEPTH):
        mod = mod_all[l]
        nw = norm_w[l].reshape(1, d)
        if l % 2 == 0:
            i = l // 2
            qwt = jnp.broadcast_to(
                jnp.tile(q_norm_w[i] * (LOG2E * ATT_HEAD_DIM ** -0.5), ATT_HEADS)[:, None],
                (ATT_WIDTH, SUB_ROWS))
            kw2 = jnp.tile(k_norm_w[i], 2).reshape(1, LANES)
            sink = jnp.repeat(LOG2E * attn_sink[i].reshape(ATT_KV_HEADS, ATT_GROUP), BLOCK, axis=1)
            w_in = jnp.concatenate(
                [_pool_fold(w_in_a[i], w_pool[i]), w_in_a[i][:, POOL_WIDTH:].astype(BF16)], axis=1)
            u, zp, qt, k, vt, za = _even_in_proj(x, mod, nw, w_in, qwt, kw2, tm=1024)
            x = _even_mix(x, mod, u, zp, qt, k, vt, za, band, icnt,
                          pool_scale[i].reshape(1, POOL_WIDTH), bias, sink,
                          w_out_a[i].astype(BF16), tq=512)
        else:
            jdx = l // 2
            n_main = 2 * GLA_KEY_WIDTH + 2 * GLA_VAL_WIDTH
            w_main = w_in_c[jdx][:, :n_main].astype(BF16)
            w_a = w_in_c[jdx][:, n_main:].astype(BF16)
            w_g = jnp.concatenate([
                jnp.concatenate([w_gate_up[jdx, 0], zeros_g], axis=1),
                jnp.concatenate([zeros_g, w_gate_up[jdx, 1]], axis=1)], axis=0).astype(BF16)
            b_g = b_gate[jdx].reshape(1, 2 * GLA_KEY_WIDTH)
            q, k, v, z, gf, gb = _odd_in_proj(x, mod, nw, w_main, w_a, w_g, b_g, tm=1024)
            x = _gla_mix(x, mod, q, k, v, z, gf, gb, gla_norm_w[jdx].reshape(1, GLA_DV),
                         w_out_c[jdx].astype(BF16), tb=512)
    return x
```

```python
import functools

import jax
import jax.numpy as jnp
from jax import lax
from jax.experimental import pallas as pl
from jax.experimental.pallas import tpu as pltpu

F32 = jnp.float32
BF16 = jnp.bfloat16

D_MODEL = 1024
DEPTH = 4
POOL_WINDOWS = (2, 4, 8, 16)
POOL_WIDTH = 512
POOL_GROUP_DIM = 128
ATT_HEADS = 8
ATT_KV_HEADS = 2
ATT_GROUP = ATT_HEADS // ATT_KV_HEADS
ATT_HEAD_DIM = 64
ATT_WIDTH = 512
ATT_KV_WIDTH = 128
WINDOW = 128
BLOCK = 128
GLA_HEADS = 4
GLA_KEY_WIDTH = 512
GLA_VAL_WIDTH = 1024
GLA_DK = 128
GLA_DV = 256
GLA_GATE_RANK = 16
GLA_GATE_NORMALIZER = 16.0
EPS = 1e-6
NEG = -1e30
LOG2E = 1.4426950408889634

LANES = 128
POOL_HALO = 64
GLA_CHUNK = 128
SUB_ROWS = 512
UNIT_HEADS = 4
SCORE_AHEAD = 0
VMEM_LIMIT = 56 * 1024 * 1024


def _dot(a, b):
    return jnp.dot(a, b, preferred_element_type=F32)


def _dot_nt(a, b):
    return lax.dot_general(a, b, (((1,), (1,)), ((), ())), preferred_element_type=F32)


def _dot_tn(a, b):
    return lax.dot_general(a, b, (((0,), (0,)), ((), ())), preferred_element_type=F32)


def _split_bf16(x):
    hi = x.astype(BF16)
    lo = (x - hi.astype(F32)).astype(BF16)
    return hi, lo


def _silu(x):
    return x * jax.nn.sigmoid(x)


def _mod_kernel(c_ref, w_ref, b_ref, o_ref):
    sc = _silu(c_ref[...])
    sc_hi, sc_lo = _split_bf16(sc)
    w_hi, w_lo = _split_bf16(w_ref[0])
    acc = _dot(sc_hi, w_hi) + _dot(sc_lo, w_hi) + _dot(sc_hi, w_lo)
    o_ref[0] = acc + b_ref[0]


def _modulation(c, w_ada, b_ada):
    depth, d, n = w_ada.shape
    bsz = c.shape[0]
    tn = 1024
    return pl.pallas_call(
        _mod_kernel,
        grid=(depth, n // tn),
        in_specs=[
            pl.BlockSpec((bsz, d), lambda l, j: (0, 0)),
            pl.BlockSpec((1, d, tn), lambda l, j: (l, 0, j)),
            pl.BlockSpec((1, 1, tn), lambda l, j: (l, 0, j)),
        ],
        out_specs=pl.BlockSpec((1, bsz, tn), lambda l, j: (l, 0, j)),
        out_shape=jax.ShapeDtypeStruct((depth, bsz, n), F32),
        compiler_params=pltpu.CompilerParams(
            dimension_semantics=("parallel", "parallel"), vmem_limit_bytes=VMEM_LIMIT),
        name="adaln_mod",
    )(c, w_ada, b_ada.reshape(depth, 1, n))


def _pool_fold_kernel(w_ref, wp_ref, o_ref):
    w_hi, w_lo = _split_bf16(w_ref[...])
    p_hi, p_lo = _split_bf16(wp_ref[0])
    o_ref[...] = (_dot(w_hi, p_hi) + _dot(w_lo, p_hi) + _dot(w_hi, p_lo)).astype(BF16)


def _pool_fold(w_in, w_pool):
    d = w_in.shape[0]
    groups, gd, _ = w_pool.shape
    return pl.pallas_call(
        _pool_fold_kernel,
        grid=(groups,),
        in_specs=[
            pl.BlockSpec((d, gd), lambda g: (0, g)),
            pl.BlockSpec((1, gd, gd), lambda g: (g, 0, 0)),
        ],
        out_specs=pl.BlockSpec((d, gd), lambda g: (0, g)),
        out_shape=jax.ShapeDtypeStruct((d, groups * gd), BF16),
        compiler_params=pltpu.CompilerParams(dimension_semantics=("parallel",)),
        name="pool_fold",
    )(w_in, w_pool)


def _norm_modulate(x, nw, mod):
    ms = jnp.mean(x * x, axis=-1, keepdims=True)
    y = (x * lax.rsqrt(ms + EPS)) * nw
    shift = mod[0:1, :]
    scale = mod[1:2, :]
    return (y * (1.0 + scale) + shift).astype(BF16)


def _head_rms(p, w2, head_dim):
    assert 2 * head_dim == LANES and p.shape[-1] == LANES
    lane = lax.broadcasted_iota(jnp.int32, (1, LANES), 1)
    first = lane < head_dim
    sq = p * p
    s0 = jnp.sum(jnp.where(first, sq, 0.0), axis=-1, keepdims=True)
    s1 = jnp.sum(jnp.where(first, 0.0, sq), axis=-1, keepdims=True)
    ms = jnp.where(first, s0, s1) * (1.0 / head_dim)
    return (p * lax.rsqrt(ms + EPS)) * w2


def _even_in_kernel(x_ref, mod_ref, nw_ref, w_ref, qwt_ref, kw_ref,
                    u_ref, zp_ref, qt_ref, k_ref, vt_ref, za_ref):
    for r in range(x_ref.shape[1] // SUB_ROWS):
        tok = slice(r * SUB_ROWS, (r + 1) * SUB_ROWS)
        hb = _norm_modulate(x_ref[0, tok, :], nw_ref[...], mod_ref[0])
        c0 = 0
        u_ref[0, tok, :] = _dot(hb, w_ref[:, c0:c0 + POOL_WIDTH]).astype(BF16)
        c0 += POOL_WIDTH
        zp_ref[0, tok, :] = _silu(_dot(hb, w_ref[:, c0:c0 + POOL_WIDTH])).astype(BF16)
        c0 += POOL_WIDTH
        pqt = _dot(hb, w_ref[:, c0:c0 + ATT_WIDTH]).T
        c0 += ATT_WIDTH
        for h in range(ATT_HEADS):
            rows = slice(h * ATT_HEAD_DIM, (h + 1) * ATT_HEAD_DIM)
            ph = pqt[rows, :]
            ms = jnp.mean(ph * ph, axis=0, keepdims=True)
            qt_ref[0, rows, tok] = ((ph * lax.rsqrt(ms + EPS)) * qwt_ref[rows, :]).astype(BF16)
        pkv = _dot(hb, w_ref[:, c0:c0 + 2 * ATT_KV_WIDTH])
        c0 += 2 * ATT_KV_WIDTH
        k_ref[0, tok, :] = _head_rms(pkv[:, 0:ATT_KV_WIDTH], kw_ref[...], ATT_HEAD_DIM).astype(BF16)
        vt_ref[0, :, tok] = pkv[:, ATT_KV_WIDTH:].T.astype(BF16)
        za_ref[0, tok, :] = _silu(_dot(hb, w_ref[:, c0:c0 + ATT_WIDTH])).astype(BF16)


def _even_in_proj(x, mod, nw, w_in, qwt, kw2, tm):
    bsz, s, d = x.shape
    n = w_in.shape[1]
    row = lambda b, i: (b, i, 0)
    col = lambda b, i: (b, 0, i)
    const2 = lambda b, i: (0, 0)
    outs = (((s, POOL_WIDTH), (tm, POOL_WIDTH), row), ((s, POOL_WIDTH), (tm, POOL_WIDTH), row),
            ((ATT_WIDTH, s), (ATT_WIDTH, tm), col), ((s, ATT_KV_WIDTH), (tm, ATT_KV_WIDTH), row),
            ((ATT_KV_WIDTH, s), (ATT_KV_WIDTH, tm), col), ((s, ATT_WIDTH), (tm, ATT_WIDTH), row))
    return pl.pallas_call(
        _even_in_kernel,
        grid=(bsz, s // tm),
        in_specs=[
            pl.BlockSpec((1, tm, d), row),
            pl.BlockSpec((1, 3, d), lambda b, i: (b, 0, 0)),
            pl.BlockSpec((1, d), const2),
            pl.BlockSpec((d, n), const2),
            pl.BlockSpec((ATT_WIDTH, SUB_ROWS), const2),
            pl.BlockSpec((1, LANES), const2),
        ],
        out_specs=[pl.BlockSpec((1,) + blk, imap) for _, blk, imap in outs],
        out_shape=[jax.ShapeDtypeStruct((bsz,) + full, BF16) for full, _, _ in outs],
        compiler_params=pltpu.CompilerParams(
            dimension_semantics=("parallel", "parallel"), vmem_limit_bytes=VMEM_LIMIT),
        name="even_in_proj",
    )(x, mod, nw, w_in, qwt, kw2)


def _even_mix_kernel(x_ref, mod_ref, u_ref, up_ref, un_ref, zp_ref, qt_ref, k_ref, kp_ref, kn_ref,
                     vt_ref, vtp_ref, vtn_ref, za_ref, band_ref, icnt_ref, ps_ref, bias_ref, sink_ref,
                     wo_ref, o_ref, ue_scr, ke_scr, vte_scr, ot_scr, mix_scr):
    tq = x_ref.shape[1]
    i = pl.program_id(1)
    last = pl.num_programs(1) - 1
    nblk = tq // BLOCK
    kspan = 3 * BLOCK

    ue_scr[0:POOL_HALO, :] = jnp.where(i > 0, up_ref[0], jnp.zeros_like(up_ref[0]))
    ue_scr[POOL_HALO:POOL_HALO + tq, :] = u_ref[0]
    ue_scr[POOL_HALO + tq:, :] = jnp.where(i < last, un_ref[0], jnp.zeros_like(un_ref[0]))
    tots = {}
    for j in range(nblk):
        for g in range(len(POOL_WINDOWS)):
            cols = slice(g * POOL_GROUP_DIM, (g + 1) * POOL_GROUP_DIM)
            tots[j, g] = _dot(band_ref[g], ue_scr[j * BLOCK:j * BLOCK + BLOCK + 2 * POOL_HALO, cols])
    for j in range(nblk):
        rows = slice(j * BLOCK, (j + 1) * BLOCK)
        for g, w in enumerate(POOL_WINDOWS):
            cols = slice(g * POOL_GROUP_DIM, (g + 1) * POOL_GROUP_DIM)
            inv_cnt = 1.0 / w
            if j == 0:
                inv_cnt = jnp.where(i == 0, icnt_ref[0, :, cols], inv_cnt)
            if j == nblk - 1:
                inv_cnt = jnp.where(i == last, icnt_ref[1, :, cols], inv_cnt)
            y = (tots[j, g] * inv_cnt - u_ref[0, rows, cols].astype(F32)) * ps_ref[:, cols]
            mix_scr[rows, cols] = (y * zp_ref[0, rows, cols].astype(F32)).astype(BF16)

    ke_scr[0:BLOCK, :] = kp_ref[0]
    ke_scr[BLOCK:BLOCK + tq, :] = k_ref[0]
    ke_scr[BLOCK + tq:, :] = kn_ref[0]
    vte_scr[:, 0:BLOCK] = vtp_ref[0]
    vte_scr[:, BLOCK:BLOCK + tq] = vt_ref[0]
    vte_scr[:, BLOCK + tq:] = vtn_ref[0]
    krow = lax.broadcasted_iota(jnp.int32, (kspan, 1), 0)
    units = [(j, kh, gp) for j in range(nblk) for kh in range(ATT_KV_HEADS)
             for gp in range(ATT_GROUP // UNIT_HEADS)]

    def scores(j, kh, gp):
        rows = slice(j * BLOCK, (j + 1) * BLOCK)
        hd = slice(kh * ATT_HEAD_DIM, (kh + 1) * ATT_HEAD_DIM)
        h0 = kh * ATT_GROUP + gp * UNIT_HEADS
        qs = jnp.concatenate(
            [qt_ref[0, (h0 + g) * ATT_HEAD_DIM:(h0 + g + 1) * ATT_HEAD_DIM, rows]
             for g in range(UNIT_HEADS)], axis=1)
        return _dot(ke_scr[j * BLOCK:j * BLOCK + kspan, hd], qs)

    pending = [scores(*units[n]) for n in range(min(SCORE_AHEAD, len(units)))]
    for n, (j, kh, gp) in enumerate(units):
        lanes = slice(gp * UNIT_HEADS * BLOCK, (gp + 1) * UNIT_HEADS * BLOCK)
        if n + SCORE_AHEAD < len(units):
            pending.append(scores(*units[n + SCORE_AHEAD]))
        st = pending.pop(0) + bias_ref[kh, :, lanes]
        if j == 0:
            st = st + jnp.where((krow < BLOCK) & (i == 0), NEG, 0.0)
        if j == nblk - 1:
            st = st + jnp.where((krow >= 2 * BLOCK) & (i == last), NEG, 0.0)
        hd = slice(kh * ATT_HEAD_DIM, (kh + 1) * ATT_HEAD_DIM)
        sink = sink_ref[kh:kh + 1, lanes]
        m = jnp.maximum(jnp.max(st, axis=0, keepdims=True), sink)
        p = jnp.exp2(st - m)
        den = jnp.sum(p, axis=0, keepdims=True) + jnp.exp2(sink - m)
        ot = _dot(vte_scr[hd, j * BLOCK:j * BLOCK + kspan], p.astype(BF16)) * (1.0 / den)
        for g in range(UNIT_HEADS):
            h = kh * ATT_GROUP + gp * UNIT_HEADS + g
            ot_scr[j, h * ATT_HEAD_DIM:(h + 1) * ATT_HEAD_DIM, :] = ot[:, g * BLOCK:(g + 1) * BLOCK]
        if n + 1 == len(units) or units[n + 1][0] != j:
            rows = slice(j * BLOCK, (j + 1) * BLOCK)
            o = ot_scr[j].T
            mix_scr[rows, POOL_WIDTH:] = (o * za_ref[0, rows, :].astype(F32)).astype(BF16)

    y = _dot(mix_scr[...], wo_ref[...])
    gate = mod_ref[0][2:3, :]
    o_ref[0] = x_ref[0] + gate * y


def _even_mix(x, mod, u, zp, qt, k, vt, za, band, icnt, pool_scale, bias, sink, w_out, tq):
    bsz, s, d = x.shape
    nt = s // tq
    row = lambda b, i: (b, i, 0)
    col = lambda b, i: (b, 0, i)
    const2 = lambda b, i: (0, 0)
    const3 = lambda b, i: (0, 0, 0)
    hp = tq // POOL_HALO
    hb = tq // BLOCK
    prev_row = lambda n: (lambda b, i: (b, jnp.maximum(i * n - 1, 0), 0))
    next_row = lambda n, tot: (lambda b, i: (b, jnp.minimum((i + 1) * n, tot - 1), 0))
    return pl.pallas_call(
        _even_mix_kernel,
        grid=(bsz, nt),
        in_specs=[
            pl.BlockSpec((1, tq, d), row),
            pl.BlockSpec((1, 3, d), lambda b, i: (b, 0, 0)),
            pl.BlockSpec((1, tq, POOL_WIDTH), row),
            pl.BlockSpec((1, POOL_HALO, POOL_WIDTH), prev_row(hp)),
            pl.BlockSpec((1, POOL_HALO, POOL_WIDTH), next_row(hp, s // POOL_HALO)),
            pl.BlockSpec((1, tq, POOL_WIDTH), row),
            pl.BlockSpec((1, ATT_WIDTH, tq), col),
            pl.BlockSpec((1, tq, ATT_KV_WIDTH), row),
            pl.BlockSpec((1, BLOCK, ATT_KV_WIDTH), prev_row(hb)),
            pl.BlockSpec((1, BLOCK, ATT_KV_WIDTH), next_row(hb, s // BLOCK)),
            pl.BlockSpec((1, ATT_KV_WIDTH, tq), col),
            pl.BlockSpec((1, ATT_KV_WIDTH, BLOCK), lambda b, i: (b, 0, jnp.maximum(i * hb - 1, 0))),
            pl.BlockSpec((1, ATT_KV_WIDTH, BLOCK), lambda b, i: (b, 0, jnp.minimum((i + 1) * hb, s // BLOCK - 1))),
            pl.BlockSpec((1, tq, ATT_WIDTH), row),
            pl.BlockSpec(band.shape, const3),
            pl.BlockSpec(icnt.shape, const3),
            pl.BlockSpec((1, POOL_WIDTH), const2),
            pl.BlockSpec(bias.shape, const3),
            pl.BlockSpec(sink.shape, const2),
            pl.BlockSpec(w_out.shape, const2),
        ],
        out_specs=pl.BlockSpec((1, tq, d), row),
        out_shape=jax.ShapeDtypeStruct((bsz, s, d), F32),
        scratch_shapes=[
            pltpu.VMEM((tq + 2 * POOL_HALO, POOL_WIDTH), BF16),
            pltpu.VMEM((tq + 2 * BLOCK, ATT_KV_WIDTH), BF16),
            pltpu.VMEM((ATT_KV_WIDTH, tq + 2 * BLOCK), BF16),
            pltpu.VMEM((tq // BLOCK, ATT_WIDTH, BLOCK), F32),
            pltpu.VMEM((tq, d), BF16),
        ],
        compiler_params=pltpu.CompilerParams(
            dimension_semantics=("parallel", "parallel"), vmem_limit_bytes=VMEM_LIMIT),
        name="even_mix",
    )(x, mod, u, u, u, zp, qt, k, k, k, vt, vt, vt, za, band, icnt, pool_scale, bias, sink, w_out)


def _attn_bias():
    r = jnp.arange(BLOCK)[:, None]
    c = jnp.arange(3 * BLOCK)[None, :]
    dist = jnp.abs(r + BLOCK - c)
    slopes = 2.0 ** (-8.0 * jnp.arange(1, ATT_HEADS + 1, dtype=F32) / ATT_HEADS)
    bias = -(LOG2E * slopes)[:, None, None] * dist.astype(F32)[None]
    bias = jnp.where((dist <= WINDOW)[None], bias, NEG)
    bias = bias.reshape(ATT_KV_HEADS, ATT_GROUP, BLOCK, 3 * BLOCK)
    return jnp.transpose(bias, (0, 3, 1, 2)).reshape(ATT_KV_HEADS, 3 * BLOCK, ATT_GROUP * BLOCK)


def _pool_band():
    r = jnp.arange(BLOCK)[:, None]
    c = jnp.arange(BLOCK + 2 * POOL_HALO)[None, :]
    off = c - POOL_HALO - r
    return jnp.stack([((off >= -(w // 2)) & (off <= w // 2 - 1)) for w in POOL_WINDOWS]).astype(BF16)


def _pool_inv_count(seq_len):
    t = jnp.concatenate([jnp.arange(BLOCK), jnp.arange(seq_len - BLOCK, seq_len)])[:, None]
    w = jnp.repeat(jnp.asarray(POOL_WINDOWS), POOL_GROUP_DIM)[None, :]
    lo = jnp.clip(t - w // 2, 0, seq_len - 1)
    hi = jnp.clip(t + w // 2 - 1, 0, seq_len - 1)
    return (1.0 / (hi - lo + 1).astype(F32)).reshape(2, BLOCK, POOL_WIDTH)


def _log_sigmoid(x):
    return jnp.minimum(x, 0.0) - jnp.log(1.0 + jnp.exp(-jnp.abs(x)))


def _odd_in_kernel(x_ref, mod_ref, nw_ref, w_ref, wa_ref, wg_ref, bg_ref,
                   q_ref, k_ref, v_ref, z_ref, gf_ref, gb_ref):
    for r in range(x_ref.shape[1] // SUB_ROWS):
        tok = slice(r * SUB_ROWS, (r + 1) * SUB_ROWS)
        hb = _norm_modulate(x_ref[0, tok, :], nw_ref[...], mod_ref[0])
        c0 = 0
        q_ref[0, tok, :] = (_dot(hb, w_ref[:, c0:c0 + GLA_KEY_WIDTH]) * (GLA_DK ** -0.5)).astype(BF16)
        c0 += GLA_KEY_WIDTH
        k_ref[0, tok, :] = _dot(hb, w_ref[:, c0:c0 + GLA_KEY_WIDTH]).astype(BF16)
        c0 += GLA_KEY_WIDTH
        v_ref[0, tok, :] = _dot(hb, w_ref[:, c0:c0 + GLA_VAL_WIDTH]).astype(BF16)
        c0 += GLA_VAL_WIDTH
        z_ref[0, tok, :] = _silu(_dot(hb, w_ref[:, c0:c0 + GLA_VAL_WIDTH])).astype(BF16)
        a = _dot(hb, wa_ref[...]).astype(BF16)
        logits = _dot(a, wg_ref[...]) + bg_ref[...]
        g = _log_sigmoid(logits) * (LOG2E / GLA_GATE_NORMALIZER)
        g_hi, g_lo = _split_bf16(g)
        gf_ref[0, tok, 0:GLA_KEY_WIDTH] = g_hi[:, 0:GLA_KEY_WIDTH]
        gf_ref[0, tok, GLA_KEY_WIDTH:] = g_lo[:, 0:GLA_KEY_WIDTH]
        gb_ref[0, tok, 0:GLA_KEY_WIDTH] = g_hi[:, GLA_KEY_WIDTH:]
        gb_ref[0, tok, GLA_KEY_WIDTH:] = g_lo[:, GLA_KEY_WIDTH:]


def _odd_in_proj(x, mod, nw, w_main, w_a, w_g, b_g, tm):
    bsz, s, d = x.shape
    row = lambda b, i: (b, i, 0)
    const2 = lambda b, i: (0, 0)
    outs = ((GLA_KEY_WIDTH, BF16), (GLA_KEY_WIDTH, BF16), (GLA_VAL_WIDTH, BF16), (GLA_VAL_WIDTH, BF16),
            (2 * GLA_KEY_WIDTH, BF16), (2 * GLA_KEY_WIDTH, BF16))
    return pl.pallas_call(
        _odd_in_kernel,
        grid=(bsz, s // tm),
        in_specs=[
            pl.BlockSpec((1, tm, d), row),
            pl.BlockSpec((1, 3, d), lambda b, i: (b, 0, 0)),
            pl.BlockSpec((1, d), const2),
            pl.BlockSpec(w_main.shape, const2),
            pl.BlockSpec(w_a.shape, const2),
            pl.BlockSpec(w_g.shape, const2),
            pl.BlockSpec(b_g.shape, const2),
        ],
        out_specs=[pl.BlockSpec((1, tm, w), row) for w, _ in outs],
        out_shape=[jax.ShapeDtypeStruct((bsz, s, w), dt) for w, dt in outs],
        compiler_params=pltpu.CompilerParams(
            dimension_semantics=("parallel", "parallel"), vmem_limit_bytes=VMEM_LIMIT),
        name="odd_in_proj",
    )(x, mod, nw, w_main, w_a, w_g, b_g)


def _cumsum_mats():
    r = jnp.arange(GLA_CHUNK)[:, None]
    c = jnp.arange(GLA_CHUNK)[None, :]
    return jnp.stack([c <= r, c >= r]).astype(BF16)


def _chunk_scan(tri, g2):
    r = _dot(tri, g2)
    return r[:, 0:GLA_KEY_WIDTH] + r[:, GLA_KEY_WIDTH:]


def _decay_cols(row):
    colb = jnp.broadcast_to(row, (GLA_DK, GLA_DK)).T
    return jnp.concatenate([colb] * (GLA_DV // GLA_DK), axis=1)


def _gla_kernel(x_ref, mod_ref, q_ref, k_ref, v_ref, z_ref, gf_ref, gb_ref, tri_ref, gw_ref, wo_ref,
                o_ref, sf_scr, sb_scr, snap_scr, beta_scr, kbd_scr, mix_scr, *, nt):
    tb = x_ref.shape[1]
    nch = tb // GLA_CHUNK
    j = pl.program_id(1)
    c = GLA_CHUNK
    lower, upper = tri_ref[0], tri_ref[1]
    causal = (lax.broadcasted_iota(jnp.int32, (c, c), 1) <= lax.broadcasted_iota(jnp.int32, (c, c), 0))

    @pl.when(j == 0)
    def _():
        sf_scr[...] = jnp.zeros_like(sf_scr)
        sb_scr[...] = jnp.zeros_like(sb_scr)
        kbd_scr[...] = jnp.zeros_like(kbd_scr)

    @pl.when(j < nt)
    def _():
        blk = nt - 1 - j
        betas = [_chunk_scan(upper, gb_ref[0, cc * c:(cc + 1) * c, :]) for cc in range(nch)]
        upd, dec = {}, {}
        for cc in range(nch - 1, -1, -1):
            rows = slice(cc * c, (cc + 1) * c)
            beta_scr[pl.ds((blk * nch + cc) * c, c), :] = betas[cc]
            for h in range(GLA_HEADS):
                kc = slice(h * GLA_DK, (h + 1) * GLA_DK)
                vc = slice(h * GLA_DV, (h + 1) * GLA_DV)
                bh = betas[cc][:, kc]
                first = bh[0:1, :]
                kx = k_ref[0, rows, kc] * jnp.exp2(first - bh).astype(BF16)
                upd[cc, h] = _dot_tn(kx, v_ref[0, rows, vc])
                dec[cc, h] = _decay_cols(jnp.exp2(first))
        for h in range(GLA_HEADS):
            st = sb_scr[h]
            for cc in range(nch - 1, -1, -1):
                snap_scr[blk * nch + cc, h] = st.astype(BF16)
                st = dec[cc, h] * st + upd[cc, h]
            sb_scr[h] = st

    @pl.when(j >= nt)
    def _():
        blk = j - nt
        bfw_next = _chunk_scan(lower, gf_ref[0, 0:c, :])
        states = [sf_scr[h] for h in range(GLA_HEADS)]
        for cc in range(nch):
            rows = slice(cc * c, (cc + 1) * c)
            bfw = bfw_next
            beta = beta_scr[pl.ds((blk * nch + cc) * c, c), :]
            a2 = []
            for h in range(GLA_HEADS):
                kc = slice(h * GLA_DK, (h + 1) * GLA_DK)
                qh = q_ref[0, rows, kc]
                kh = k_ref[0, rows, kc]
                bf = bfw[:, kc]
                bb = beta[:, kc]
                rf = bf[c // 2:c // 2 + 1, :]
                rb = bb[c // 2 - 1:c // 2, :]
                qcat = jnp.concatenate(
                    [qh * jnp.exp2(bf - rf).astype(BF16), qh * jnp.exp2(bb - rb).astype(BF16)], axis=1)
                kbd_scr[h, 0:c, 0:GLA_DK] = kh * jnp.exp2(rf - bf).astype(BF16)
                kbd_scr[h, c:2 * c, GLA_DK:2 * GLA_DK] = kh * jnp.exp2(rb - bb).astype(BF16)
                a2.append(_dot_nt(qcat, kbd_scr[h]))
            upd = []
            for h in range(GLA_HEADS):
                kc = slice(h * GLA_DK, (h + 1) * GLA_DK)
                vc = slice(h * GLA_DV, (h + 1) * GLA_DV)
                bf = bfw[:, kc]
                kx = k_ref[0, rows, kc] * jnp.exp2(bf[c - 1:c, :] - bf).astype(BF16)
                upd.append(_dot_tn(kx, v_ref[0, rows, vc]))
            if cc + 1 < nch:
                bfw_next = _chunk_scan(lower, gf_ref[0, (cc + 1) * c:(cc + 2) * c, :])
            outs = []
            for h in range(GLA_HEADS):
                kc = slice(h * GLA_DK, (h + 1) * GLA_DK)
                vc = slice(h * GLA_DV, (h + 1) * GLA_DV)
                bf = bfw[:, kc]
                bb = beta[:, kc]
                att = jnp.where(causal, a2[h][:, 0:c], a2[h][:, c:2 * c]).astype(BF16)
                qh = q_ref[0, rows, kc]
                lhs = jnp.concatenate(
                    [att, qh * jnp.exp2(bf).astype(BF16), qh * jnp.exp2(bb).astype(BF16)], axis=1)
                rhs = jnp.concatenate(
                    [v_ref[0, rows, vc], states[h].astype(BF16), snap_scr[blk * nch + cc, h]], axis=0)
                outs.append(_dot(lhs, rhs))
                states[h] = _decay_cols(jnp.exp2(bf[c - 1:c, :])) * states[h] + upd[h]
            for h in range(GLA_HEADS):
                vc = slice(h * GLA_DV, (h + 1) * GLA_DV)
                o = outs[h]
                ms = jnp.mean(o * o, axis=-1, keepdims=True)
                on = (o * lax.rsqrt(ms + EPS)) * gw_ref[...]
                mix_scr[rows, vc] = on.astype(BF16) * z_ref[0, rows, vc]
        for h in range(GLA_HEADS):
            sf_scr[h] = states[h]
        y = _dot(mix_scr[...], wo_ref[...])
        gate = mod_ref[0][2:3, :]
        o_ref[0] = x_ref[0] + gate * y


def _gla_mix(x, mod, q, k, v, z, gf, gb, gw, w_out, tb):
    bsz, s, d = x.shape
    nt = s // tb
    const2 = lambda b, j: (0, 0)
    both = lambda b, j: (b, jnp.where(j < nt, nt - 1 - j, j - nt), 0)
    fwd_only = lambda b, j: (b, jnp.maximum(j - nt, 0), 0)
    return pl.pallas_call(
        functools.partial(_gla_kernel, nt=nt),
        grid=(bsz, 2 * nt),
        in_specs=[
            pl.BlockSpec((1, tb, d), fwd_only),
            pl.BlockSpec((1, 3, d), lambda b, j: (b, 0, 0)),
            pl.BlockSpec((1, tb, GLA_KEY_WIDTH), fwd_only),
            pl.BlockSpec((1, tb, GLA_KEY_WIDTH), both),
            pl.BlockSpec((1, tb, GLA_VAL_WIDTH), both),
            pl.BlockSpec((1, tb, GLA_VAL_WIDTH), fwd_only),
            pl.BlockSpec((1, tb, 2 * GLA_KEY_WIDTH), fwd_only),
            pl.BlockSpec((1, tb, 2 * GLA_KEY_WIDTH), lambda b, j: (b, jnp.maximum(nt - 1 - j, 0), 0)),
            pl.BlockSpec((2, GLA_CHUNK, GLA_CHUNK), lambda b, j: (0, 0, 0)),
            pl.BlockSpec((1, GLA_DV), const2),
            pl.BlockSpec(w_out.shape, const2),
        ],
        out_specs=pl.BlockSpec((1, tb, d), fwd_only),
        out_shape=jax.ShapeDtypeStruct((bsz, s, d), F32),
        scratch_shapes=[
            pltpu.VMEM((GLA_HEADS, GLA_DK, GLA_DV), F32),
            pltpu.VMEM((GLA_HEADS, GLA_DK, GLA_DV), F32),
            pltpu.VMEM((s // GLA_CHUNK, GLA_HEADS, GLA_DK, GLA_DV), BF16),
            pltpu.VMEM((s, GLA_KEY_WIDTH), F32),
            pltpu.VMEM((GLA_HEADS, 2 * GLA_CHUNK, 2 * GLA_DK), BF16),
            pltpu.VMEM((tb, d), BF16),
        ],
        compiler_params=pltpu.CompilerParams(
            dimension_semantics=("parallel", "arbitrary"), vmem_limit_bytes=VMEM_LIMIT),
        name="gla_mix",
    )(x, mod, q, k, v, z, gf, gb, _cumsum_mats(), gw, w_out)


def kernel(x, c, norm_w, w_ada, b_ada, w_in_a, w_pool, pool_scale, q_norm_w, k_norm_w, attn_sink,
           w_out_a, w_in_c, w_gate_up, b_gate, gla_norm_w, w_out_c):
    bsz, s, d = x.shape
    assert d == D_MODEL and s % 512 == 0
    mod_all = _modulation(c, w_ada, b_ada).reshape(DEPTH, bsz, 3, d)
    bias = _attn_bias()
    band = _pool_band()
    icnt = _pool_inv_count(s)
    zeros_g = jnp.zeros((GLA_GATE_RANK, GLA_KEY_WIDTH), F32)
    for l in range(DEPTH):
        mod = mod_all[l]
        nw = norm_w[l].reshape(1, d)
        if l % 2 == 0:
            i = l // 2
            qwt = jnp.broadcast_to(
                jnp.tile(q_norm_w[i] * (LOG2E * ATT_HEAD_DIM ** -0.5), ATT_HEADS)[:, None],
                (ATT_WIDTH, SUB_ROWS))
            kw2 = jnp.tile(k_norm_w[i], 2).reshape(1, LANES)
            sink = jnp.repeat(LOG2E * attn_sink[i].reshape(ATT_KV_HEADS, ATT_GROUP), BLOCK, axis=1)
            w_in = jnp.concatenate(
                [_pool_fold(w_in_a[i], w_pool[i]), w_in_a[i][:, POOL_WIDTH:].astype(BF16)], axis=1)
            u, zp, qt, k, vt, za = _even_in_proj(x, mod, nw, w_in, qwt, kw2, tm=1024)
            x = _even_mix(x, mod, u, zp, qt, k, vt, za, band, icnt,
                          pool_scale[i].reshape(1, POOL_WIDTH), bias, sink,
                          w_out_a[i].astype(BF16), tq=512)
        else:
            jdx = l // 2
            n_main = 2 * GLA_KEY_WIDTH + 2 * GLA_VAL_WIDTH
            w_main = w_in_c[jdx][:, :n_main].astype(BF16)
            w_a = w_in_c[jdx][:, n_main:].astype(BF16)
            w_g = jnp.concatenate([
                jnp.concatenate([w_gate_up[jdx, 0], zeros_g], axis=1),
                jnp.concatenate([zeros_g, w_gate_up[jdx, 1]], axis=1)], axis=0).astype(BF16)
            b_g = b_gate[jdx].reshape(1, 2 * GLA_KEY_WIDTH)
            q, k, v, z, gf, gb = _odd_in_proj(x, mod, nw, w_main, w_a, w_g, b_g, tm=1024)
            x = _gla_mix(x, mod, q, k, v, z, gf, gb, gla_norm_w[jdx].reshape(1, GLA_DV),
                         w_out_c[jdx].astype(BF16), tb=512)
    return x
```

```python
import functools

import jax
import jax.numpy as jnp
from jax import lax
from jax.experimental import pallas as pl
from jax.experimental.pallas import tpu as pltpu

F32 = jnp.float32
BF16 = jnp.bfloat16

D_MODEL = 1024
DEPTH = 4
POOL_WINDOWS = (2, 4, 8, 16)
POOL_WIDTH = 512
POOL_GROUP_DIM = 128
ATT_HEADS = 8
ATT_KV_HEADS = 2
ATT_GROUP = ATT_HEADS // ATT_KV_HEADS
ATT_HEAD_DIM = 64
ATT_WIDTH = 512
ATT_KV_WIDTH = 128
WINDOW = 128
BLOCK = 128
GLA_HEADS = 4
GLA_KEY_WIDTH = 512
GLA_VAL_WIDTH = 1024
GLA_DK = 128
GLA_DV = 256
GLA_GATE_RANK = 16
GLA_GATE_NORMALIZER = 16.0
EPS = 1e-6
NEG = -1e30
LOG2E = 1.4426950408889634

LANES = 128
POOL_HALO = 64
GLA_CHUNK = 128
SUB_ROWS = 512
ONES_ROWS = 16
VMEM_LIMIT = 56 * 1024 * 1024


def _dot(a, b):
    return jnp.dot(a, b, preferred_element_type=F32)


def _dot_nt(a, b):
    return lax.dot_general(a, b, (((1,), (1,)), ((), ())), preferred_element_type=F32)


def _dot_tn(a, b):
    return lax.dot_general(a, b, (((0,), (0,)), ((), ())), preferred_element_type=F32)


def _split_bf16(x):
    hi = x.astype(BF16)
    lo = (x - hi.astype(F32)).astype(BF16)
    return hi, lo


def _silu(x):
    return x * jax.nn.sigmoid(x)


def _mod_kernel(c_ref, w_ref, b_ref, o_ref):
    sc = _silu(c_ref[...])
    sc_hi, sc_lo = _split_bf16(sc)
    w_hi, w_lo = _split_bf16(w_ref[0])
    acc = _dot(sc_hi, w_hi) + _dot(sc_lo, w_hi) + _dot(sc_hi, w_lo)
    o_ref[0] = acc + b_ref[0]


def _modulation(c, w_ada, b_ada):
    depth, d, n = w_ada.shape
    bsz = c.shape[0]
    tn = 1024
    return pl.pallas_call(
        _mod_kernel,
        grid=(depth, n // tn),
        in_specs=[
            pl.BlockSpec((bsz, d), lambda l, j: (0, 0)),
            pl.BlockSpec((1, d, tn), lambda l, j: (l, 0, j)),
            pl.BlockSpec((1, 1, tn), lambda l, j: (l, 0, j)),
        ],
        out_specs=pl.BlockSpec((1, bsz, tn), lambda l, j: (l, 0, j)),
        out_shape=jax.ShapeDtypeStruct((depth, bsz, n), F32),
        compiler_params=pltpu.CompilerParams(
            dimension_semantics=("parallel", "parallel"), vmem_limit_bytes=VMEM_LIMIT),
        name="adaln_mod",
    )(c, w_ada, b_ada.reshape(depth, 1, n))


def _pool_fold_kernel(w_ref, wp_ref, o_ref):
    w_hi, w_lo = _split_bf16(w_ref[...])
    p_hi, p_lo = _split_bf16(wp_ref[0])
    o_ref[...] = (_dot(w_hi, p_hi) + _dot(w_lo, p_hi) + _dot(w_hi, p_lo)).astype(BF16)


def _pool_fold(w_in, w_pool):
    d = w_in.shape[0]
    groups, gd, _ = w_pool.shape
    return pl.pallas_call(
        _pool_fold_kernel,
        grid=(groups,),
        in_specs=[
            pl.BlockSpec((d, gd), lambda g: (0, g)),
            pl.BlockSpec((1, gd, gd), lambda g: (g, 0, 0)),
        ],
        out_specs=pl.BlockSpec((d, gd), lambda g: (0, g)),
        out_shape=jax.ShapeDtypeStruct((d, groups * gd), BF16),
        compiler_params=pltpu.CompilerParams(dimension_semantics=("parallel",)),
        name="pool_fold",
    )(w_in, w_pool)


def _two_stage_pipeline(n, produce, consume):
    always = pl.program_id(0) >= 0
    produce(0)
    for r in range(n):
        @pl.when(always)
        def _(r=r):
            if r + 1 < n:
                produce(r + 1)
            consume(r)


def _norm_modulate(x, nw, mod):
    ms = jnp.mean(x * x, axis=-1, keepdims=True)
    y = (x * lax.rsqrt(ms + EPS)) * nw
    shift = mod[0:1, :]
    scale = mod[1:2, :]
    return (y * (1.0 + scale) + shift).astype(BF16)


def _head_rms(p, w2, head_dim):
    assert 2 * head_dim == LANES and p.shape[-1] == LANES
    lane = lax.broadcasted_iota(jnp.int32, (1, LANES), 1)
    first = lane < head_dim
    sq = p * p
    s0 = jnp.sum(jnp.where(first, sq, 0.0), axis=-1, keepdims=True)
    s1 = jnp.sum(jnp.where(first, 0.0, sq), axis=-1, keepdims=True)
    ms = jnp.where(first, s0, s1) * (1.0 / head_dim)
    return (p * lax.rsqrt(ms + EPS)) * w2


def _even_in_kernel(x_ref, mod_ref, nw_ref, w_ref, qwt_ref, kw_ref,
                    u_ref, zp_ref, qt_ref, k_ref, vt_ref, za_ref):
    c_u, c_zp, c_q = 0, POOL_WIDTH, 2 * POOL_WIDTH
    c_kv = c_q + ATT_WIDTH
    c_za = c_kv + 2 * ATT_KV_WIDTH
    for r in range(x_ref.shape[1] // SUB_ROWS):
        tok = slice(r * SUB_ROWS, (r + 1) * SUB_ROWS)
        hb = _norm_modulate(x_ref[0, tok, :], nw_ref[...], mod_ref[0])
        pqt = _dot(hb, w_ref[:, c_q:c_q + ATT_WIDTH]).T
        for h in range(ATT_HEADS):
            rows = slice(h * ATT_HEAD_DIM, (h + 1) * ATT_HEAD_DIM)
            ph = pqt[rows, :]
            ms = jnp.mean(ph * ph, axis=0, keepdims=True)
            qt_ref[0, rows, tok] = ((ph * lax.rsqrt(ms + EPS)) * qwt_ref[rows, :]).astype(BF16)
        pkv = _dot(hb, w_ref[:, c_kv:c_kv + 2 * ATT_KV_WIDTH])
        k_ref[0, tok, :] = _head_rms(pkv[:, 0:ATT_KV_WIDTH], kw_ref[...], ATT_HEAD_DIM).astype(BF16)
        vt_ref[0, :, tok] = pkv[:, ATT_KV_WIDTH:].T.astype(BF16)
        zp_ref[0, tok, :] = _silu(_dot(hb, w_ref[:, c_zp:c_zp + POOL_WIDTH])).astype(BF16)
        za_ref[0, tok, :] = _silu(_dot(hb, w_ref[:, c_za:c_za + ATT_WIDTH])).astype(BF16)
        u_ref[0, tok, :] = _dot(hb, w_ref[:, c_u:c_u + POOL_WIDTH]).astype(BF16)


def _even_in_proj(x, mod, nw, w_in, qwt, kw2, tm):
    bsz, s, d = x.shape
    n = w_in.shape[1]
    row = lambda b, i: (b, i, 0)
    col = lambda b, i: (b, 0, i)
    const2 = lambda b, i: (0, 0)
    outs = (((s, POOL_WIDTH), (tm, POOL_WIDTH), row), ((s, POOL_WIDTH), (tm, POOL_WIDTH), row),
            ((ATT_WIDTH, s), (ATT_WIDTH, tm), col), ((s, ATT_KV_WIDTH), (tm, ATT_KV_WIDTH), row),
            ((ATT_KV_WIDTH, s), (ATT_KV_WIDTH, tm), col), ((s, ATT_WIDTH), (tm, ATT_WIDTH), row))
    return pl.pallas_call(
        _even_in_kernel,
        grid=(bsz, s // tm),
        in_specs=[
            pl.BlockSpec((1, tm, d), row),
            pl.BlockSpec((1, 3, d), lambda b, i: (b, 0, 0)),
            pl.BlockSpec((1, d), const2),
            pl.BlockSpec((d, n), const2),
            pl.BlockSpec((ATT_WIDTH, SUB_ROWS), const2),
            pl.BlockSpec((1, LANES), const2),
        ],
        out_specs=[pl.BlockSpec((1,) + blk, imap) for _, blk, imap in outs],
        out_shape=[jax.ShapeDtypeStruct((bsz,) + full, BF16) for full, _, _ in outs],
        compiler_params=pltpu.CompilerParams(
            dimension_semantics=("parallel", "parallel"), vmem_limit_bytes=VMEM_LIMIT),
        name="even_in_proj",
    )(x, mod, nw, w_in, qwt, kw2)


def _even_mix_kernel(x_ref, mod_ref, u_ref, up_ref, un_ref, zp_ref, qt_ref, k_ref, kp_ref, kn_ref,
                     vt_ref, vtp_ref, vtn_ref, za_ref, band_ref, icnt_ref, ps_ref, bias_ref, sink_ref,
                     wo_ref, o_ref, ue_scr, ke_scr, vte_scr, st_scr, m_scr, ot_scr, mix_scr):
    tq = x_ref.shape[1]
    i = pl.program_id(1)
    last = pl.num_programs(1) - 1
    nblk = tq // BLOCK
    kspan = 3 * BLOCK

    ue_scr[0:POOL_HALO, :] = jnp.where(i > 0, up_ref[0], jnp.zeros_like(up_ref[0]))
    ue_scr[POOL_HALO:POOL_HALO + tq, :] = u_ref[0]
    ue_scr[POOL_HALO + tq:, :] = jnp.where(i < last, un_ref[0], jnp.zeros_like(un_ref[0]))
    tots = {}
    for j in range(nblk):
        for g in range(len(POOL_WINDOWS)):
            cols = slice(g * POOL_GROUP_DIM, (g + 1) * POOL_GROUP_DIM)
            tots[j, g] = _dot(band_ref[g], ue_scr[j * BLOCK:j * BLOCK + BLOCK + 2 * POOL_HALO, cols])
    for j in range(nblk):
        rows = slice(j * BLOCK, (j + 1) * BLOCK)
        for g, w in enumerate(POOL_WINDOWS):
            cols = slice(g * POOL_GROUP_DIM, (g + 1) * POOL_GROUP_DIM)
            inv_cnt = 1.0 / w
            if j == 0:
                inv_cnt = jnp.where(i == 0, icnt_ref[0, :, cols], inv_cnt)
            if j == nblk - 1:
                inv_cnt = jnp.where(i == last, icnt_ref[1, :, cols], inv_cnt)
            y = (tots[j, g] * inv_cnt - u_ref[0, rows, cols].astype(F32)) * ps_ref[:, cols]
            mix_scr[rows, cols] = (y * zp_ref[0, rows, cols].astype(F32)).astype(BF16)

    ke_scr[0:BLOCK, :] = kp_ref[0]
    ke_scr[BLOCK:BLOCK + tq, :] = k_ref[0]
    ke_scr[BLOCK + tq:, :] = kn_ref[0]
    for kh in range(ATT_KV_HEADS):
        hd = slice(kh * ATT_HEAD_DIM, (kh + 1) * ATT_HEAD_DIM)
        vte_scr[kh, 0:ATT_HEAD_DIM, 0:BLOCK] = vtp_ref[0, hd, :]
        vte_scr[kh, 0:ATT_HEAD_DIM, BLOCK:BLOCK + tq] = vt_ref[0, hd, :]
        vte_scr[kh, 0:ATT_HEAD_DIM, BLOCK + tq:] = vtn_ref[0, hd, :]
        vte_scr[kh, ATT_HEAD_DIM:, :] = jnp.ones((ONES_ROWS, tq + 2 * BLOCK), BF16)

    def score_stage(j):
        rows = slice(j * BLOCK, (j + 1) * BLOCK)
        top = jnp.where(i == 0, 3, 0) if j == 0 else 0
        bot = jnp.where(i == last, 3, 2) if j == nblk - 1 else 2
        for kh in range(ATT_KV_HEADS):
            hd = slice(kh * ATT_HEAD_DIM, (kh + 1) * ATT_HEAD_DIM)
            qs = jnp.concatenate(
                [qt_ref[0, (kh * ATT_GROUP + g) * ATT_HEAD_DIM:(kh * ATT_GROUP + g + 1) * ATT_HEAD_DIM, rows]
                 for g in range(ATT_GROUP)], axis=1)
            qk = _dot(ke_scr[j * BLOCK:j * BLOCK + kspan, hd], qs)
            st = jnp.concatenate(
                [qk[kb * BLOCK:(kb + 1) * BLOCK] + bias_ref[kh * 4 + blk]
                 for kb, blk in enumerate((top, 1, bot))], axis=0)
            st_scr[j % 2, kh] = st
            m_scr[j % 2, kh] = jnp.maximum(jnp.max(st, axis=0, keepdims=True), sink_ref[kh:kh + 1, :])

    def value_stage(j):
        rows = slice(j * BLOCK, (j + 1) * BLOCK)
        for kh in range(ATT_KV_HEADS):
            m = m_scr[j % 2, kh]
            p = jnp.exp2(st_scr[j % 2, kh] - m).astype(BF16)
            pv = _dot(vte_scr[kh, :, j * BLOCK:j * BLOCK + kspan], p)
            den = pv[ATT_HEAD_DIM:ATT_HEAD_DIM + 1, :] + jnp.exp2(sink_ref[kh:kh + 1, :] - m)
            ot = pv[0:ATT_HEAD_DIM, :] * (1.0 / den)
            for g in range(ATT_GROUP):
                h = kh * ATT_GROUP + g
                ot_scr[h * ATT_HEAD_DIM:(h + 1) * ATT_HEAD_DIM, :] = ot[:, g * BLOCK:(g + 1) * BLOCK]
        o = ot_scr[...].T
        mix_scr[rows, POOL_WIDTH:] = (o * za_ref[0, rows, :].astype(F32)).astype(BF16)

    _two_stage_pipeline(nblk, score_stage, value_stage)

    @pl.when(pl.program_id(0) >= 0)
    def _():
        y = _dot(mix_scr[...], wo_ref[...])
        gate = mod_ref[0][2:3, :]
        o_ref[0] = x_ref[0] + gate * y


def _even_mix(x, mod, u, zp, qt, k, vt, za, band, icnt, pool_scale, bias, sink, w_out, tq):
    bsz, s, d = x.shape
    nt = s // tq
    row = lambda b, i: (b, i, 0)
    col = lambda b, i: (b, 0, i)
    const2 = lambda b, i: (0, 0)
    const3 = lambda b, i: (0, 0, 0)
    hp = tq // POOL_HALO
    hb = tq // BLOCK
    prev_row = lambda n: (lambda b, i: (b, jnp.maximum(i * n - 1, 0), 0))
    next_row = lambda n, tot: (lambda b, i: (b, jnp.minimum((i + 1) * n, tot - 1), 0))
    return pl.pallas_call(
        _even_mix_kernel,
        grid=(bsz, nt),
        in_specs=[
            pl.BlockSpec((1, tq, d), row),
            pl.BlockSpec((1, 3, d), lambda b, i: (b, 0, 0)),
            pl.BlockSpec((1, tq, POOL_WIDTH), row),
            pl.BlockSpec((1, POOL_HALO, POOL_WIDTH), prev_row(hp)),
            pl.BlockSpec((1, POOL_HALO, POOL_WIDTH), next_row(hp, s // POOL_HALO)),
            pl.BlockSpec((1, tq, POOL_WIDTH), row),
            pl.BlockSpec((1, ATT_WIDTH, tq), col),
            pl.BlockSpec((1, tq, ATT_KV_WIDTH), row),
            pl.BlockSpec((1, BLOCK, ATT_KV_WIDTH), prev_row(hb)),
            pl.BlockSpec((1, BLOCK, ATT_KV_WIDTH), next_row(hb, s // BLOCK)),
            pl.BlockSpec((1, ATT_KV_WIDTH, tq), col),
            pl.BlockSpec((1, ATT_KV_WIDTH, BLOCK), lambda b, i: (b, 0, jnp.maximum(i * hb - 1, 0))),
            pl.BlockSpec((1, ATT_KV_WIDTH, BLOCK), lambda b, i: (b, 0, jnp.minimum((i + 1) * hb, s // BLOCK - 1))),
            pl.BlockSpec((1, tq, ATT_WIDTH), row),
            pl.BlockSpec(band.shape, const3),
            pl.BlockSpec(icnt.shape, const3),
            pl.BlockSpec((1, POOL_WIDTH), const2),
            pl.BlockSpec(bias.shape, const3),
            pl.BlockSpec(sink.shape, const2),
            pl.BlockSpec(w_out.shape, const2),
        ],
        out_specs=pl.BlockSpec((1, tq, d), row),
        out_shape=jax.ShapeDtypeStruct((bsz, s, d), F32),
        scratch_shapes=[
            pltpu.VMEM((tq + 2 * POOL_HALO, POOL_WIDTH), BF16),
            pltpu.VMEM((tq + 2 * BLOCK, ATT_KV_WIDTH), BF16),
            pltpu.VMEM((ATT_KV_HEADS, ATT_HEAD_DIM + ONES_ROWS, tq + 2 * BLOCK), BF16),
            pltpu.VMEM((2, ATT_KV_HEADS, 3 * BLOCK, ATT_GROUP * BLOCK), F32),
            pltpu.VMEM((2, ATT_KV_HEADS, 1, ATT_GROUP * BLOCK), F32),
            pltpu.VMEM((ATT_WIDTH, BLOCK), F32),
            pltpu.VMEM((tq, d), BF16),
        ],
        compiler_params=pltpu.CompilerParams(
            dimension_semantics=("parallel", "parallel"), vmem_limit_bytes=VMEM_LIMIT),
        name="even_mix",
    )(x, mod, u, u, u, zp, qt, k, k, k, vt, vt, vt, za, band, icnt, pool_scale, bias, sink, w_out)


def _attn_bias():
    r = jnp.arange(BLOCK)[:, None]
    c = jnp.arange(3 * BLOCK)[None, :]
    dist = jnp.abs(r + BLOCK - c)
    slopes = 2.0 ** (-8.0 * jnp.arange(1, ATT_HEADS + 1, dtype=F32) / ATT_HEADS)
    bias = -(LOG2E * slopes)[:, None, None] * dist.astype(F32)[None]
    bias = jnp.where((dist <= WINDOW)[None], bias, NEG)
    bias = bias.reshape(ATT_KV_HEADS, ATT_GROUP, BLOCK, 3, BLOCK)
    bias = jnp.transpose(bias, (0, 3, 4, 1, 2)).reshape(ATT_KV_HEADS, 3, BLOCK, ATT_GROUP * BLOCK)
    masked = jnp.full((ATT_KV_HEADS, 1, BLOCK, ATT_GROUP * BLOCK), NEG, F32)
    return jnp.concatenate([bias, masked], axis=1).reshape(ATT_KV_HEADS * 4, BLOCK, ATT_GROUP * BLOCK)


def _pool_band():
    r = jnp.arange(BLOCK)[:, None]
    c = jnp.arange(BLOCK + 2 * POOL_HALO)[None, :]
    off = c - POOL_HALO - r
    return jnp.stack([((off >= -(w // 2)) & (off <= w // 2 - 1)) for w in POOL_WINDOWS]).astype(BF16)


def _pool_inv_count(seq_len):
    t = jnp.concatenate([jnp.arange(BLOCK), jnp.arange(seq_len - BLOCK, seq_len)])[:, None]
    w = jnp.repeat(jnp.asarray(POOL_WINDOWS), POOL_GROUP_DIM)[None, :]
    lo = jnp.clip(t - w // 2, 0, seq_len - 1)
    hi = jnp.clip(t + w // 2 - 1, 0, seq_len - 1)
    return (1.0 / (hi - lo + 1).astype(F32)).reshape(2, BLOCK, POOL_WIDTH)


def _log_sigmoid(x):
    return jnp.minimum(x, 0.0) - jnp.log(1.0 + jnp.exp(-jnp.abs(x)))


def _odd_in_kernel(x_ref, mod_ref, nw_ref, w_ref, wg_ref, bg_ref,
                   q_ref, k_ref, v_ref, z_ref, gf_ref, gb_ref):
    c_q, c_k, c_v = 0, GLA_KEY_WIDTH, 2 * GLA_KEY_WIDTH
    c_z = c_v + GLA_VAL_WIDTH
    c_a = c_z + GLA_VAL_WIDTH
    for r in range(x_ref.shape[1] // SUB_ROWS):
        tok = slice(r * SUB_ROWS, (r + 1) * SUB_ROWS)
        hb = _norm_modulate(x_ref[0, tok, :], nw_ref[...], mod_ref[0])
        q_ref[0, tok, :] = (_dot(hb, w_ref[:, c_q:c_q + GLA_KEY_WIDTH]) * (GLA_DK ** -0.5)).astype(BF16)
        k_ref[0, tok, :] = _dot(hb, w_ref[:, c_k:c_k + GLA_KEY_WIDTH]).astype(BF16)
        v_ref[0, tok, :] = _dot(hb, w_ref[:, c_v:c_v + GLA_VAL_WIDTH]).astype(BF16)
        z_ref[0, tok, :] = _silu(_dot(hb, w_ref[:, c_z:c_z + GLA_VAL_WIDTH])).astype(BF16)
        a = _dot(hb, w_ref[:, c_a:c_a + 2 * GLA_GATE_RANK]).astype(BF16)
        logits = _dot(a, wg_ref[...]) + bg_ref[...]
        g = _log_sigmoid(logits) * (LOG2E / GLA_GATE_NORMALIZER)
        g_hi, g_lo = _split_bf16(g)
        gf_ref[0, tok, 0:GLA_KEY_WIDTH] = g_hi[:, 0:GLA_KEY_WIDTH]
        gf_ref[0, tok, GLA_KEY_WIDTH:] = g_lo[:, 0:GLA_KEY_WIDTH]
        gb_ref[0, tok, 0:GLA_KEY_WIDTH] = g_hi[:, GLA_KEY_WIDTH:]
        gb_ref[0, tok, GLA_KEY_WIDTH:] = g_lo[:, GLA_KEY_WIDTH:]


def _odd_in_proj(x, mod, nw, w_in, w_g, b_g, tm):
    bsz, s, d = x.shape
    row = lambda b, i: (b, i, 0)
    const2 = lambda b, i: (0, 0)
    outs = ((GLA_KEY_WIDTH, BF16), (GLA_KEY_WIDTH, BF16), (GLA_VAL_WIDTH, BF16), (GLA_VAL_WIDTH, BF16),
            (2 * GLA_KEY_WIDTH, BF16), (2 * GLA_KEY_WIDTH, BF16))
    return pl.pallas_call(
        _odd_in_kernel,
        grid=(bsz, s // tm),
        in_specs=[
            pl.BlockSpec((1, tm, d), row),
            pl.BlockSpec((1, 3, d), lambda b, i: (b, 0, 0)),
            pl.BlockSpec((1, d), const2),
            pl.BlockSpec(w_in.shape, const2),
            pl.BlockSpec(w_g.shape, const2),
            pl.BlockSpec(b_g.shape, const2),
        ],
        out_specs=[pl.BlockSpec((1, tm, w), row) for w, _ in outs],
        out_shape=[jax.ShapeDtypeStruct((bsz, s, w), dt) for w, dt in outs],
        compiler_params=pltpu.CompilerParams(
            dimension_semantics=("parallel", "parallel"), vmem_limit_bytes=VMEM_LIMIT),
        name="odd_in_proj",
    )(x, mod, nw, w_in, w_g, b_g)


def _cumsum_mats():
    r = jnp.arange(GLA_CHUNK)[:, None]
    c = jnp.arange(GLA_CHUNK)[None, :]
    return jnp.stack([c <= r, c >= r]).astype(BF16)


def _chunk_scan(tri, g2):
    r = _dot(tri, g2)
    return r[:, 0:GLA_KEY_WIDTH] + r[:, GLA_KEY_WIDTH:]


def _decay_cols(row):
    colb = jnp.broadcast_to(row, (GLA_DK, GLA_DK)).T
    return jnp.concatenate([colb] * (GLA_DV // GLA_DK), axis=1)


def _gla_kernel(x_ref, mod_ref, q_ref, k_ref, v_ref, z_ref, gf_ref, gb_ref, tri_ref, gw_ref, wo_ref,
                o_ref, sf_scr, sb_scr, snap_scr, beta_scr, kbd_scr, mix_scr, *, nt):
    tb = x_ref.shape[1]
    nch = tb // GLA_CHUNK
    j = pl.program_id(1)
    c = GLA_CHUNK
    lower, upper = tri_ref[0], tri_ref[1]
    causal = (lax.broadcasted_iota(jnp.int32, (c, c), 1) <= lax.broadcasted_iota(jnp.int32, (c, c), 0))

    @pl.when(j == 0)
    def _():
        sf_scr[...] = jnp.zeros_like(sf_scr)
        sb_scr[...] = jnp.zeros_like(sb_scr)
        kbd_scr[...] = jnp.zeros_like(kbd_scr)

    @pl.when(j < nt)
    def _():
        blk = nt - 1 - j
        betas = [_chunk_scan(upper, gb_ref[0, cc * c:(cc + 1) * c, :]) for cc in range(nch)]
        upd, dec = {}, {}
        for cc in range(nch - 1, -1, -1):
            rows = slice(cc * c, (cc + 1) * c)
            beta_scr[pl.ds((blk * nch + cc) * c, c), :] = betas[cc]
            for h in range(GLA_HEADS):
                kc = slice(h * GLA_DK, (h + 1) * GLA_DK)
                vc = slice(h * GLA_DV, (h + 1) * GLA_DV)
                bh = betas[cc][:, kc]
                first = bh[0:1, :]
                kx = k_ref[0, rows, kc] * jnp.exp2(first - bh).astype(BF16)
                upd[cc, h] = _dot_tn(kx, v_ref[0, rows, vc])
                dec[cc, h] = _decay_cols(jnp.exp2(first))
        for h in range(GLA_HEADS):
            st = sb_scr[h]
            for cc in range(nch - 1, -1, -1):
                snap_scr[blk * nch + cc, h] = st.astype(BF16)
                st = dec[cc, h] * st + upd[cc, h]
            sb_scr[h] = st

    @pl.when(j >= nt)
    def _():
        blk = j - nt
        bfw_next = _chunk_scan(lower, gf_ref[0, 0:c, :])
        states = [sf_scr[h] for h in range(GLA_HEADS)]
        for cc in range(nch):
            rows = slice(cc * c, (cc + 1) * c)
            bfw = bfw_next
            beta = beta_scr[pl.ds((blk * nch + cc) * c, c), :]
            a2 = []
            for h in range(GLA_HEADS):
                kc = slice(h * GLA_DK, (h + 1) * GLA_DK)
                qh = q_ref[0, rows, kc]
                kh = k_ref[0, rows, kc]
                bf = bfw[:, kc]
                bb = beta[:, kc]
                rf = bf[c // 2:c // 2 + 1, :]
                rb = bb[c // 2 - 1:c // 2, :]
                qcat = jnp.concatenate(
                    [qh * jnp.exp2(bf - rf).astype(BF16), qh * jnp.exp2(bb - rb).astype(BF16)], axis=1)
                kbd_scr[h, 0:c, 0:GLA_DK] = kh * jnp.exp2(rf - bf).astype(BF16)
                kbd_scr[h, c:2 * c, GLA_DK:2 * GLA_DK] = kh * jnp.exp2(rb - bb).astype(BF16)
                a2.append(_dot_nt(qcat, kbd_scr[h]))
            upd = []
            for h in range(GLA_HEADS):
                kc = slice(h * GLA_DK, (h + 1) * GLA_DK)
                vc = slice(h * GLA_DV, (h + 1) * GLA_DV)
                bf = bfw[:, kc]
                kx = k_ref[0, rows, kc] * jnp.exp2(bf[c - 1:c, :] - bf).astype(BF16)
                upd.append(_dot_tn(kx, v_ref[0, rows, vc]))
            if cc + 1 < nch:
                bfw_next = _chunk_scan(lower, gf_ref[0, (cc + 1) * c:(cc + 2) * c, :])
            outs = []
            for h in range(GLA_HEADS):
                kc = slice(h * GLA_DK, (h + 1) * GLA_DK)
                vc = slice(h * GLA_DV, (h + 1) * GLA_DV)
                bf = bfw[:, kc]
                bb = beta[:, kc]
                att = jnp.where(causal, a2[h][:, 0:c], a2[h][:, c:2 * c]).astype(BF16)
                qh = q_ref[0, rows, kc]
                lhs = jnp.concatenate(
                    [att, qh * jnp.exp2(bf).astype(BF16), qh * jnp.exp2(bb).astype(BF16)], axis=1)
                rhs = jnp.concatenate(
                    [v_ref[0, rows, vc], states[h].astype(BF16), snap_scr[blk * nch + cc, h]], axis=0)
                outs.append(_dot(lhs, rhs))
                states[h] = _decay_cols(jnp.exp2(bf[c - 1:c, :])) * states[h] + upd[h]
            for h in range(GLA_HEADS):
                vc = slice(h * GLA_DV, (h + 1) * GLA_DV)
                o = outs[h]
                ms = jnp.mean(o * o, axis=-1, keepdims=True)
                on = (o * lax.rsqrt(ms + EPS)) * gw_ref[...]
                mix_scr[rows, vc] = on.astype(BF16) * z_ref[0, rows, vc]
        for h in range(GLA_HEADS):
            sf_scr[h] = states[h]
        y = _dot(mix_scr[...], wo_ref[...])
        gate = mod_ref[0][2:3, :]
        o_ref[0] = x_ref[0] + gate * y


def _gla_mix(x, mod, q, k, v, z, gf, gb, gw, w_out, tb):
    bsz, s, d = x.shape
    nt = s // tb
    const2 = lambda b, j: (0, 0)
    both = lambda b, j: (b, jnp.where(j < nt, nt - 1 - j, j - nt), 0)
    fwd_only = lambda b, j: (b, jnp.maximum(j - nt, 0), 0)
    return pl.pallas_call(
        functools.partial(_gla_kernel, nt=nt),
        grid=(bsz, 2 * nt),
        in_specs=[
            pl.BlockSpec((1, tb, d), fwd_only),
            pl.BlockSpec((1, 3, d), lambda b, j: (b, 0, 0)),
            pl.BlockSpec((1, tb, GLA_KEY_WIDTH), fwd_only),
            pl.BlockSpec((1, tb, GLA_KEY_WIDTH), both),
            pl.BlockSpec((1, tb, GLA_VAL_WIDTH), both),
            pl.BlockSpec((1, tb, GLA_VAL_WIDTH), fwd_only),
            pl.BlockSpec((1, tb, 2 * GLA_KEY_WIDTH), fwd_only),
            pl.BlockSpec((1, tb, 2 * GLA_KEY_WIDTH), lambda b, j: (b, jnp.maximum(nt - 1 - j, 0), 0)),
            pl.BlockSpec((2, GLA_CHUNK, GLA_CHUNK), lambda b, j: (0, 0, 0)),
            pl.BlockSpec((1, GLA_DV), const2),
            pl.BlockSpec(w_out.shape, const2),
        ],
        out_specs=pl.BlockSpec((1, tb, d), fwd_only),
        out_shape=jax.ShapeDtypeStruct((bsz, s, d), F32),
        scratch_shapes=[
            pltpu.VMEM((GLA_HEADS, GLA_DK, GLA_DV), F32),
            pltpu.VMEM((GLA_HEADS, GLA_DK, GLA_DV), F32),
            pltpu.VMEM((s // GLA_CHUNK, GLA_HEADS, GLA_DK, GLA_DV), BF16),
            pltpu.VMEM((s, GLA_KEY_WIDTH), F32),
            pltpu.VMEM((GLA_HEADS, 2 * GLA_CHUNK, 2 * GLA_DK), BF16),
            pltpu.VMEM((tb, d), BF16),
        ],
        compiler_params=pltpu.CompilerParams(
            dimension_semantics=("parallel", "arbitrary"), vmem_limit_bytes=VMEM_LIMIT),
        name="gla_mix",
    )(x, mod, q, k, v, z, gf, gb, _cumsum_mats(), gw, w_out)


def kernel(x, c, norm_w, w_ada, b_ada, w_in_a, w_pool, pool_scale, q_norm_w, k_norm_w, attn_sink,
           w_out_a, w_in_c, w_gate_up, b_gate, gla_norm_w, w_out_c):
    bsz, s, d = x.shape
    assert d == D_MODEL and s % 512 == 0
    mod_all = _modulation(c, w_ada, b_ada).reshape(DEPTH, bsz, 3, d)
    bias = _attn_bias()
    band = _pool_band()
    icnt = _pool_inv_count(s)
    zeros_g = jnp.zeros((GLA_GATE_RANK, GLA_KEY_WIDTH), F32)
    for l in range(DEPTH):
        mod = mod_all[l]
        nw = norm_w[l].reshape(1, d)
        if l % 2 == 0:
            i = l // 2
            qwt = jnp.broadcast_to(
                jnp.tile(q_norm_w[i] * (LOG2E * ATT_HEAD_DIM ** -0.5), ATT_HEADS)[:, None],
                (ATT_WIDTH, SUB_ROWS))
            kw2 = jnp.tile(k_norm_w[i], 2).reshape(1, LANES)
            sink = jnp.repeat(LOG2E * attn_sink[i].reshape(ATT_KV_HEADS, ATT_GROUP), BLOCK, axis=1)
            w_in = jnp.concatenate(
                [_pool_fold(w_in_a[i], w_pool[i]), w_in_a[i][:, POOL_WIDTH:].astype(BF16)], axis=1)
            u, zp, qt, k, vt, za = _even_in_proj(x, mod, nw, w_in, qwt, kw2, tm=1024)
            x = _even_mix(x, mod, u, zp, qt, k, vt, za, band, icnt,
                          pool_scale[i].reshape(1, POOL_WIDTH), bias, sink,
                          w_out_a[i].astype(BF16), tq=512)
        else:
            jdx = l // 2
            w_g = jnp.concatenate([
                jnp.concatenate([w_gate_up[jdx, 0], zeros_g], axis=1),
                jnp.concatenate([zeros_g, w_gate_up[jdx, 1]], axis=1)], axis=0).astype(BF16)
            b_g = b_gate[jdx].reshape(1, 2 * GLA_KEY_WIDTH)
            q, k, v, z, gf, gb = _odd_in_proj(x, mod, nw, w_in_c[jdx].astype(BF16), w_g, b_g, tm=1024)
            x = _gla_mix(x, mod, q, k, v, z, gf, gb, gla_norm_w[jdx].reshape(1, GLA_DV),
                         w_out_c[jdx].astype(BF16), tb=512)
    return x
```

```python
import functools

import jax
import jax.numpy as jnp
from jax import lax
from jax.experimental import pallas as pl
from jax.experimental.pallas import tpu as pltpu

F32 = jnp.float32
BF16 = jnp.bfloat16

D_MODEL = 1024
DEPTH = 4
POOL_WINDOWS = (2, 4, 8, 16)
POOL_WIDTH = 512
POOL_GROUP_DIM = 128
ATT_HEADS = 8
ATT_KV_HEADS = 2
ATT_GROUP = ATT_HEADS // ATT_KV_HEADS
ATT_HEAD_DIM = 64
ATT_WIDTH = 512
ATT_KV_WIDTH = 128
WINDOW = 128
BLOCK = 128
GLA_HEADS = 4
GLA_KEY_WIDTH = 512
GLA_VAL_WIDTH = 1024
GLA_DK = 128
GLA_DV = 256
GLA_GATE_RANK = 16
GLA_GATE_NORMALIZER = 16.0
EPS = 1e-6
NEG = -1e30
LOG2E = 1.4426950408889634

LANES = 128
POOL_HALO = 64
GLA_CHUNK = 128
SUB_ROWS = 512
ONES_ROWS = 16
VMEM_LIMIT = 56 * 1024 * 1024


def _dot(a, b):
    return jnp.dot(a, b, preferred_element_type=F32)


def _dot_nt(a, b):
    return lax.dot_general(a, b, (((1,), (1,)), ((), ())), preferred_element_type=F32)


def _dot_tn(a, b):
    return lax.dot_general(a, b, (((0,), (0,)), ((), ())), preferred_element_type=F32)


def _split_bf16(x):
    hi = x.astype(BF16)
    lo = (x - hi.astype(F32)).astype(BF16)
    return hi, lo


def _silu(x):
    return x * jax.nn.sigmoid(x)


def _mod_kernel(c_ref, w_ref, b_ref, o_ref):
    sc = _silu(c_ref[...])
    sc_hi, sc_lo = _split_bf16(sc)
    w_hi, w_lo = _split_bf16(w_ref[0])
    acc = _dot(sc_hi, w_hi) + _dot(sc_lo, w_hi) + _dot(sc_hi, w_lo)
    o_ref[0] = acc + b_ref[0]


def _modulation(c, w_ada, b_ada):
    depth, d, n = w_ada.shape
    bsz = c.shape[0]
    tn = 1024
    return pl.pallas_call(
        _mod_kernel,
        grid=(depth, n // tn),
        in_specs=[
            pl.BlockSpec((bsz, d), lambda l, j: (0, 0)),
            pl.BlockSpec((1, d, tn), lambda l, j: (l, 0, j)),
            pl.BlockSpec((1, 1, tn), lambda l, j: (l, 0, j)),
        ],
        out_specs=pl.BlockSpec((1, bsz, tn), lambda l, j: (l, 0, j)),
        out_shape=jax.ShapeDtypeStruct((depth, bsz, n), F32),
        compiler_params=pltpu.CompilerParams(
            dimension_semantics=("parallel", "parallel"), vmem_limit_bytes=VMEM_LIMIT),
        name="adaln_mod",
    )(c, w_ada, b_ada.reshape(depth, 1, n))


def _pool_fold_kernel(w_ref, wp_ref, o_ref):
    w_hi, w_lo = _split_bf16(w_ref[...])
    p_hi, p_lo = _split_bf16(wp_ref[0])
    o_ref[...] = (_dot(w_hi, p_hi) + _dot(w_lo, p_hi) + _dot(w_hi, p_lo)).astype(BF16)


def _pool_fold(w_in, w_pool):
    d = w_in.shape[0]
    groups, gd, _ = w_pool.shape
    return pl.pallas_call(
        _pool_fold_kernel,
        grid=(groups,),
        in_specs=[
            pl.BlockSpec((d, gd), lambda g: (0, g)),
            pl.BlockSpec((1, gd, gd), lambda g: (g, 0, 0)),
        ],
        out_specs=pl.BlockSpec((d, gd), lambda g: (0, g)),
        out_shape=jax.ShapeDtypeStruct((d, groups * gd), BF16),
        compiler_params=pltpu.CompilerParams(dimension_semantics=("parallel",)),
        name="pool_fold",
    )(w_in, w_pool)


def _two_stage_pipeline(n, produce, consume):
    always = pl.program_id(0) >= 0
    produce(0)
    for r in range(n):
        @pl.when(always)
        def _(r=r):
            if r + 1 < n:
                produce(r + 1)
            consume(r)


def _norm_modulate(x, nw, mod):
    ms = jnp.mean(x * x, axis=-1, keepdims=True)
    y = (x * lax.rsqrt(ms + EPS)) * nw
    shift = mod[0:1, :]
    scale = mod[1:2, :]
    return (y * (1.0 + scale) + shift).astype(BF16)


def _head_rms(p, w2, head_dim):
    assert 2 * head_dim == LANES and p.shape[-1] == LANES
    lane = lax.broadcasted_iota(jnp.int32, (1, LANES), 1)
    first = lane < head_dim
    sq = p * p
    s0 = jnp.sum(jnp.where(first, sq, 0.0), axis=-1, keepdims=True)
    s1 = jnp.sum(jnp.where(first, 0.0, sq), axis=-1, keepdims=True)
    ms = jnp.where(first, s0, s1) * (1.0 / head_dim)
    return (p * lax.rsqrt(ms + EPS)) * w2


def _even_in_kernel(x_ref, mod_ref, nw_ref, w_ref, qwt_ref, kw_ref,
                    u_ref, zp_ref, qt_ref, k_ref, vt_ref, za_ref):
    c_u, c_zp, c_q = 0, POOL_WIDTH, 2 * POOL_WIDTH
    c_kv = c_q + ATT_WIDTH
    c_za = c_kv + 2 * ATT_KV_WIDTH
    for r in range(x_ref.shape[1] // SUB_ROWS):
        tok = slice(r * SUB_ROWS, (r + 1) * SUB_ROWS)
        hb = _norm_modulate(x_ref[0, tok, :], nw_ref[...], mod_ref[0])
        pqt = _dot(hb, w_ref[:, c_q:c_q + ATT_WIDTH]).T
        for h in range(ATT_HEADS):
            rows = slice(h * ATT_HEAD_DIM, (h + 1) * ATT_HEAD_DIM)
            ph = pqt[rows, :]
            ms = jnp.mean(ph * ph, axis=0, keepdims=True)
            qt_ref[0, rows, tok] = ((ph * lax.rsqrt(ms + EPS)) * qwt_ref[rows, :]).astype(BF16)
        pkv = _dot(hb, w_ref[:, c_kv:c_kv + 2 * ATT_KV_WIDTH])
        k_ref[0, tok, :] = _head_rms(pkv[:, 0:ATT_KV_WIDTH], kw_ref[...], ATT_HEAD_DIM).astype(BF16)
        vt_ref[0, :, tok] = pkv[:, ATT_KV_WIDTH:].T.astype(BF16)
        zp_ref[0, tok, :] = _silu(_dot(hb, w_ref[:, c_zp:c_zp + POOL_WIDTH])).astype(BF16)
        za_ref[0, tok, :] = _silu(_dot(hb, w_ref[:, c_za:c_za + ATT_WIDTH])).astype(BF16)
        u_ref[0, tok, :] = _dot(hb, w_ref[:, c_u:c_u + POOL_WIDTH]).astype(BF16)


def _even_in_proj(x, mod, nw, w_in, qwt, kw2, tm):
    bsz, s, d = x.shape
    n = w_in.shape[1]
    row = lambda b, i: (b, i, 0)
    col = lambda b, i: (b, 0, i)
    const2 = lambda b, i: (0, 0)
    outs = (((s, POOL_WIDTH), (tm, POOL_WIDTH), row), ((s, POOL_WIDTH), (tm, POOL_WIDTH), row),
            ((ATT_WIDTH, s), (ATT_WIDTH, tm), col), ((s, ATT_KV_WIDTH), (tm, ATT_KV_WIDTH), row),
            ((ATT_KV_WIDTH, s), (ATT_KV_WIDTH, tm), col), ((s, ATT_WIDTH), (tm, ATT_WIDTH), row))
    return pl.pallas_call(
        _even_in_kernel,
        grid=(bsz, s // tm),
        in_specs=[
            pl.BlockSpec((1, tm, d), row),
            pl.BlockSpec((1, 3, d), lambda b, i: (b, 0, 0)),
            pl.BlockSpec((1, d), const2),
            pl.BlockSpec((d, n), const2),
            pl.BlockSpec((ATT_WIDTH, SUB_ROWS), const2),
            pl.BlockSpec((1, LANES), const2),
        ],
        out_specs=[pl.BlockSpec((1,) + blk, imap) for _, blk, imap in outs],
        out_shape=[jax.ShapeDtypeStruct((bsz,) + full, BF16) for full, _, _ in outs],
        compiler_params=pltpu.CompilerParams(
            dimension_semantics=("parallel", "parallel"), vmem_limit_bytes=VMEM_LIMIT),
        name="even_in_proj",
    )(x, mod, nw, w_in, qwt, kw2)


def _even_mix_kernel(x_ref, mod_ref, u_ref, up_ref, un_ref, zp_ref, qt_ref, k_ref, kp_ref, kn_ref,
                     vt_ref, vtp_ref, vtn_ref, za_ref, band_ref, icnt_ref, ps_ref, bias_ref, sink_ref,
                     wo_ref, o_ref, ue_scr, ke_scr, vte_scr, st_scr, m_scr, ot_scr, mix_scr):
    tq = x_ref.shape[1]
    i = pl.program_id(1)
    last = pl.num_programs(1) - 1
    nblk = tq // BLOCK
    kspan = 3 * BLOCK

    ue_scr[0:POOL_HALO, :] = jnp.where(i > 0, up_ref[0], jnp.zeros_like(up_ref[0]))
    ue_scr[POOL_HALO:POOL_HALO + tq, :] = u_ref[0]
    ue_scr[POOL_HALO + tq:, :] = jnp.where(i < last, un_ref[0], jnp.zeros_like(un_ref[0]))
    tots = {}
    for j in range(nblk):
        for g in range(len(POOL_WINDOWS)):
            cols = slice(g * POOL_GROUP_DIM, (g + 1) * POOL_GROUP_DIM)
            tots[j, g] = _dot(band_ref[g], ue_scr[j * BLOCK:j * BLOCK + BLOCK + 2 * POOL_HALO, cols])
    for j in range(nblk):
        rows = slice(j * BLOCK, (j + 1) * BLOCK)
        for g, w in enumerate(POOL_WINDOWS):
            cols = slice(g * POOL_GROUP_DIM, (g + 1) * POOL_GROUP_DIM)
            inv_cnt = 1.0 / w
            if j == 0:
                inv_cnt = jnp.where(i == 0, icnt_ref[0, :, cols], inv_cnt)
            if j == nblk - 1:
                inv_cnt = jnp.where(i == last, icnt_ref[1, :, cols], inv_cnt)
            y = (tots[j, g] * inv_cnt - u_ref[0, rows, cols].astype(F32)) * ps_ref[:, cols]
            mix_scr[rows, cols] = (y * zp_ref[0, rows, cols].astype(F32)).astype(BF16)

    ke_scr[0:BLOCK, :] = kp_ref[0]
    ke_scr[BLOCK:BLOCK + tq, :] = k_ref[0]
    ke_scr[BLOCK + tq:, :] = kn_ref[0]
    for kh in range(ATT_KV_HEADS):
        hd = slice(kh * ATT_HEAD_DIM, (kh + 1) * ATT_HEAD_DIM)
        vte_scr[kh, 0:ATT_HEAD_DIM, 0:BLOCK] = vtp_ref[0, hd, :]
        vte_scr[kh, 0:ATT_HEAD_DIM, BLOCK:BLOCK + tq] = vt_ref[0, hd, :]
        vte_scr[kh, 0:ATT_HEAD_DIM, BLOCK + tq:] = vtn_ref[0, hd, :]
        vte_scr[kh, ATT_HEAD_DIM:, :] = jnp.ones((ONES_ROWS, tq + 2 * BLOCK), BF16)

    def score_stage(j):
        rows = slice(j * BLOCK, (j + 1) * BLOCK)
        top = jnp.where(i == 0, 3, 0) if j == 0 else 0
        bot = jnp.where(i == last, 3, 2) if j == nblk - 1 else 2
        for kh in range(ATT_KV_HEADS):
            hd = slice(kh * ATT_HEAD_DIM, (kh + 1) * ATT_HEAD_DIM)
            qs = jnp.concatenate(
                [qt_ref[0, (kh * ATT_GROUP + g) * ATT_HEAD_DIM:(kh * ATT_GROUP + g + 1) * ATT_HEAD_DIM, rows]
                 for g in range(ATT_GROUP)], axis=1)
            qk = _dot(ke_scr[j * BLOCK:j * BLOCK + kspan, hd], qs)
            st = jnp.concatenate(
                [qk[kb * BLOCK:(kb + 1) * BLOCK] + bias_ref[kh * 4 + blk]
                 for kb, blk in enumerate((top, 1, bot))], axis=0)
            st_scr[j % 2, kh] = st
            m_scr[j % 2, kh] = jnp.maximum(jnp.max(st, axis=0, keepdims=True), sink_ref[kh:kh + 1, :])

    def value_stage(j):
        rows = slice(j * BLOCK, (j + 1) * BLOCK)
        for kh in range(ATT_KV_HEADS):
            m = m_scr[j % 2, kh]
            p = jnp.exp2(st_scr[j % 2, kh] - m).astype(BF16)
            pv = _dot(vte_scr[kh, :, j * BLOCK:j * BLOCK + kspan], p)
            den = pv[ATT_HEAD_DIM:ATT_HEAD_DIM + 1, :] + jnp.exp2(sink_ref[kh:kh + 1, :] - m)
            ot = pv[0:ATT_HEAD_DIM, :] * (1.0 / den)
            for g in range(ATT_GROUP):
                h = kh * ATT_GROUP + g
                ot_scr[h * ATT_HEAD_DIM:(h + 1) * ATT_HEAD_DIM, :] = ot[:, g * BLOCK:(g + 1) * BLOCK]
        o = ot_scr[...].T
        mix_scr[rows, POOL_WIDTH:] = (o * za_ref[0, rows, :].astype(F32)).astype(BF16)

    _two_stage_pipeline(nblk, score_stage, value_stage)

    @pl.when(pl.program_id(0) >= 0)
    def _():
        y = _dot(mix_scr[...], wo_ref[...])
        gate = mod_ref[0][2:3, :]
        o_ref[0] = x_ref[0] + gate * y


def _even_mix(x, mod, u, zp, qt, k, vt, za, band, icnt, pool_scale, bias, sink, w_out, tq):
    bsz, s, d = x.shape
    nt = s // tq
    row = lambda b, i: (b, i, 0)
    col = lambda b, i: (b, 0, i)
    const2 = lambda b, i: (0, 0)
    const3 = lambda b, i: (0, 0, 0)
    hp = tq // POOL_HALO
    hb = tq // BLOCK
    prev_row = lambda n: (lambda b, i: (b, jnp.maximum(i * n - 1, 0), 0))
    next_row = lambda n, tot: (lambda b, i: (b, jnp.minimum((i + 1) * n, tot - 1), 0))
    return pl.pallas_call(
        _even_mix_kernel,
        grid=(bsz, nt),
        in_specs=[
            pl.BlockSpec((1, tq, d), row),
            pl.BlockSpec((1, 3, d), lambda b, i: (b, 0, 0)),
            pl.BlockSpec((1, tq, POOL_WIDTH), row),
            pl.BlockSpec((1, POOL_HALO, POOL_WIDTH), prev_row(hp)),
            pl.BlockSpec((1, POOL_HALO, POOL_WIDTH), next_row(hp, s // POOL_HALO)),
            pl.BlockSpec((1, tq, POOL_WIDTH), row),
            pl.BlockSpec((1, ATT_WIDTH, tq), col),
            pl.BlockSpec((1, tq, ATT_KV_WIDTH), row),
            pl.BlockSpec((1, BLOCK, ATT_KV_WIDTH), prev_row(hb)),
            pl.BlockSpec((1, BLOCK, ATT_KV_WIDTH), next_row(hb, s // BLOCK)),
            pl.BlockSpec((1, ATT_KV_WIDTH, tq), col),
            pl.BlockSpec((1, ATT_KV_WIDTH, BLOCK), lambda b, i: (b, 0, jnp.maximum(i * hb - 1, 0))),
            pl.BlockSpec((1, ATT_KV_WIDTH, BLOCK), lambda b, i: (b, 0, jnp.minimum((i + 1) * hb, s // BLOCK - 1))),
            pl.BlockSpec((1, tq, ATT_WIDTH), row),
            pl.BlockSpec(band.shape, const3),
            pl.BlockSpec(icnt.shape, const3),
            pl.BlockSpec((1, POOL_WIDTH), const2),
            pl.BlockSpec(bias.shape, const3),
            pl.BlockSpec(sink.shape, const2),
            pl.BlockSpec(w_out.shape, const2),
        ],
        out_specs=pl.BlockSpec((1, tq, d), row),
        out_shape=jax.ShapeDtypeStruct((bsz, s, d), F32),
        scratch_shapes=[
            pltpu.VMEM((tq + 2 * POOL_HALO, POOL_WIDTH), BF16),
            pltpu.VMEM((tq + 2 * BLOCK, ATT_KV_WIDTH), BF16),
            pltpu.VMEM((ATT_KV_HEADS, ATT_HEAD_DIM + ONES_ROWS, tq + 2 * BLOCK), BF16),
            pltpu.VMEM((2, ATT_KV_HEADS, 3 * BLOCK, ATT_GROUP * BLOCK), F32),
            pltpu.VMEM((2, ATT_KV_HEADS, 1, ATT_GROUP * BLOCK), F32),
            pltpu.VMEM((ATT_WIDTH, BLOCK), F32),
            pltpu.VMEM((tq, d), BF16),
        ],
        compiler_params=pltpu.CompilerParams(
            dimension_semantics=("parallel", "parallel"), vmem_limit_bytes=VMEM_LIMIT),
        name="even_mix",
    )(x, mod, u, u, u, zp, qt, k, k, k, vt, vt, vt, za, band, icnt, pool_scale, bias, sink, w_out)


def _attn_bias():
    r = jnp.arange(BLOCK)[:, None]
    c = jnp.arange(3 * BLOCK)[None, :]
    dist = jnp.abs(r + BLOCK - c)
    slopes = 2.0 ** (-8.0 * jnp.arange(1, ATT_HEADS + 1, dtype=F32) / ATT_HEADS)
    bias = -(LOG2E * slopes)[:, None, None] * dist.astype(F32)[None]
    bias = jnp.where((dist <= WINDOW)[None], bias, NEG)
    bias = bias.reshape(ATT_KV_HEADS, ATT_GROUP, BLOCK, 3, BLOCK)
    bias = jnp.transpose(bias, (0, 3, 4, 1, 2)).reshape(ATT_KV_HEADS, 3, BLOCK, ATT_GROUP * BLOCK)
    masked = jnp.full((ATT_KV_HEADS, 1, BLOCK, ATT_GROUP * BLOCK), NEG, F32)
    return jnp.concatenate([bias, masked], axis=1).reshape(ATT_KV_HEADS * 4, BLOCK, ATT_GROUP * BLOCK)


def _pool_band():
    r = jnp.arange(BLOCK)[:, None]
    c = jnp.arange(BLOCK + 2 * POOL_HALO)[None, :]
    off = c - POOL_HALO - r
    return jnp.stack([((off >= -(w // 2)) & (off <= w // 2 - 1)) for w in POOL_WINDOWS]).astype(BF16)


def _pool_inv_count(seq_len):
    t = jnp.concatenate([jnp.arange(BLOCK), jnp.arange(seq_len - BLOCK, seq_len)])[:, None]
    w = jnp.repeat(jnp.asarray(POOL_WINDOWS), POOL_GROUP_DIM)[None, :]
    lo = jnp.clip(t - w // 2, 0, seq_len - 1)
    hi = jnp.clip(t + w // 2 - 1, 0, seq_len - 1)
    return (1.0 / (hi - lo + 1).astype(F32)).reshape(2, BLOCK, POOL_WIDTH)


def _log_sigmoid(x):
    return jnp.minimum(x, 0.0) - jnp.log(1.0 + jnp.exp(-jnp.abs(x)))


def _odd_in_kernel(x_ref, mod_ref, nw_ref, w_ref, wg_ref, bg_ref,
                   q_ref, k_ref, v_ref, z_ref, gf_ref, gb_ref):
    c_q, c_k, c_v = 0, GLA_KEY_WIDTH, 2 * GLA_KEY_WIDTH
    c_z = c_v + GLA_VAL_WIDTH
    c_a = c_z + GLA_VAL_WIDTH
    for r in range(x_ref.shape[1] // SUB_ROWS):
        tok = slice(r * SUB_ROWS, (r + 1) * SUB_ROWS)
        hb = _norm_modulate(x_ref[0, tok, :], nw_ref[...], mod_ref[0])
        q_ref[0, tok, :] = (_dot(hb, w_ref[:, c_q:c_q + GLA_KEY_WIDTH]) * (GLA_DK ** -0.5)).astype(BF16)
        k_ref[0, tok, :] = _dot(hb, w_ref[:, c_k:c_k + GLA_KEY_WIDTH]).astype(BF16)
        v_ref[0, tok, :] = _dot(hb, w_ref[:, c_v:c_v + GLA_VAL_WIDTH]).astype(BF16)
        z_ref[0, tok, :] = _silu(_dot(hb, w_ref[:, c_z:c_z + GLA_VAL_WIDTH])).astype(BF16)
        a = _dot(hb, w_ref[:, c_a:c_a + 2 * GLA_GATE_RANK]).astype(BF16)
        logits = _dot(a, wg_ref[...]) + bg_ref[...]
        g = _log_sigmoid(logits) * (LOG2E / GLA_GATE_NORMALIZER)
        gf_ref[0, tok, :] = g[:, 0:GLA_KEY_WIDTH].astype(BF16)
        gb_ref[0, tok, :] = g[:, GLA_KEY_WIDTH:].astype(BF16)


def _odd_in_proj(x, mod, nw, w_in, w_g, b_g, tm):
    bsz, s, d = x.shape
    row = lambda b, i: (b, i, 0)
    const2 = lambda b, i: (0, 0)
    outs = ((GLA_KEY_WIDTH, BF16), (GLA_KEY_WIDTH, BF16), (GLA_VAL_WIDTH, BF16), (GLA_VAL_WIDTH, BF16),
            (GLA_KEY_WIDTH, BF16), (GLA_KEY_WIDTH, BF16))
    return pl.pallas_call(
        _odd_in_kernel,
        grid=(bsz, s // tm),
        in_specs=[
            pl.BlockSpec((1, tm, d), row),
            pl.BlockSpec((1, 3, d), lambda b, i: (b, 0, 0)),
            pl.BlockSpec((1, d), const2),
            pl.BlockSpec(w_in.shape, const2),
            pl.BlockSpec(w_g.shape, const2),
            pl.BlockSpec(b_g.shape, const2),
        ],
        out_specs=[pl.BlockSpec((1, tm, w), row) for w, _ in outs],
        out_shape=[jax.ShapeDtypeStruct((bsz, s, w), dt) for w, dt in outs],
        compiler_params=pltpu.CompilerParams(
            dimension_semantics=("parallel", "parallel"), vmem_limit_bytes=VMEM_LIMIT),
        name="odd_in_proj",
    )(x, mod, nw, w_in, w_g, b_g)


def _cumsum_mats():
    r = jnp.arange(GLA_CHUNK)[:, None]
    c = jnp.arange(GLA_CHUNK)[None, :]
    return jnp.stack([c <= r, c >= r]).astype(BF16)


def _chunk_scan(tri, g):
    return _dot(tri, g)


def _decay_cols(row):
    colb = jnp.broadcast_to(row, (GLA_DK, GLA_DK)).T
    return jnp.concatenate([colb] * (GLA_DV // GLA_DK), axis=1)


def _gla_kernel(x_ref, mod_ref, q_ref, k_ref, v_ref, z_ref, gf_ref, gb_ref, tri_ref, gw_ref, wo_ref,
                o_ref, sf_scr, sb_scr, snap_scr, beta_scr, kbd_scr, mix_scr, *, nt):
    tb = x_ref.shape[1]
    nch = tb // GLA_CHUNK
    j = pl.program_id(1)
    c = GLA_CHUNK
    lower, upper = tri_ref[0], tri_ref[1]
    causal = (lax.broadcasted_iota(jnp.int32, (c, c), 1) <= lax.broadcasted_iota(jnp.int32, (c, c), 0))

    @pl.when(j == 0)
    def _():
        sf_scr[...] = jnp.zeros_like(sf_scr)
        sb_scr[...] = jnp.zeros_like(sb_scr)
        kbd_scr[...] = jnp.zeros_like(kbd_scr)

    @pl.when(j < nt)
    def _():
        blk = nt - 1 - j
        betas = [_chunk_scan(upper, gb_ref[0, cc * c:(cc + 1) * c, :]) for cc in range(nch)]
        upd, dec = {}, {}
        for cc in range(nch - 1, -1, -1):
            rows = slice(cc * c, (cc + 1) * c)
            beta_scr[pl.ds((blk * nch + cc) * c, c), :] = betas[cc]
            for h in range(GLA_HEADS):
                kc = slice(h * GLA_DK, (h + 1) * GLA_DK)
                vc = slice(h * GLA_DV, (h + 1) * GLA_DV)
                bh = betas[cc][:, kc]
                first = bh[0:1, :]
                kx = k_ref[0, rows, kc] * jnp.exp2(first - bh).astype(BF16)
                upd[cc, h] = _dot_tn(kx, v_ref[0, rows, vc])
                dec[cc, h] = _decay_cols(jnp.exp2(first))
        for h in range(GLA_HEADS):
            st = sb_scr[h]
            for cc in range(nch - 1, -1, -1):
                snap_scr[blk * nch + cc, h] = st.astype(BF16)
                st = dec[cc, h] * st + upd[cc, h]
            sb_scr[h] = st

    @pl.when(j >= nt)
    def _():
        blk = j - nt
        bfw_next = _chunk_scan(lower, gf_ref[0, 0:c, :])
        states = [sf_scr[h] for h in range(GLA_HEADS)]
        for cc in range(nch):
            rows = slice(cc * c, (cc + 1) * c)
            bfw = bfw_next
            beta = beta_scr[pl.ds((blk * nch + cc) * c, c), :]
            a2 = []
            for h in range(GLA_HEADS):
                kc = slice(h * GLA_DK, (h + 1) * GLA_DK)
                qh = q_ref[0, rows, kc]
                kh = k_ref[0, rows, kc]
                bf = bfw[:, kc]
                bb = beta[:, kc]
                rf = bf[c // 2:c // 2 + 1, :]
                rb = bb[c // 2 - 1:c // 2, :]
                qcat = jnp.concatenate(
                    [qh * jnp.exp2(bf - rf).astype(BF16), qh * jnp.exp2(bb - rb).astype(BF16)], axis=1)
                kbd_scr[h, 0:c, 0:GLA_DK] = kh * jnp.exp2(rf - bf).astype(BF16)
                kbd_scr[h, c:2 * c, GLA_DK:2 * GLA_DK] = kh * jnp.exp2(rb - bb).astype(BF16)
                a2.append(_dot_nt(qcat, kbd_scr[h]))
            upd = []
            for h in range(GLA_HEADS):
                kc = slice(h * GLA_DK, (h + 1) * GLA_DK)
                vc = slice(h * GLA_DV, (h + 1) * GLA_DV)
                bf = bfw[:, kc]
                kx = k_ref[0, rows, kc] * jnp.exp2(bf[c - 1:c, :] - bf).astype(BF16)
                upd.append(_dot_tn(kx, v_ref[0, rows, vc]))
            if cc + 1 < nch:
                bfw_next = _chunk_scan(lower, gf_ref[0, (cc + 1) * c:(cc + 2) * c, :])
            outs = []
            for h in range(GLA_HEADS):
                kc = slice(h * GLA_DK, (h + 1) * GLA_DK)
                vc = slice(h * GLA_DV, (h + 1) * GLA_DV)
                bf = bfw[:, kc]
                bb = beta[:, kc]
                att = jnp.where(causal, a2[h][:, 0:c], a2[h][:, c:2 * c]).astype(BF16)
                qh = q_ref[0, rows, kc]
                lhs = jnp.concatenate(
                    [att, qh * jnp.exp2(bf).astype(BF16), qh * jnp.exp2(bb).astype(BF16)], axis=1)
                rhs = jnp.concatenate(
                    [v_ref[0, rows, vc], states[h].astype(BF16), snap_scr[blk * nch + cc, h]], axis=0)
                outs.append(_dot(lhs, rhs))
                states[h] = _decay_cols(jnp.exp2(bf[c - 1:c, :])) * states[h] + upd[h]
            for h in range(GLA_HEADS):
                vc = slice(h * GLA_DV, (h + 1) * GLA_DV)
                o = outs[h]
                ms = jnp.mean(o * o, axis=-1, keepdims=True)
                on = (o * lax.rsqrt(ms + EPS)) * gw_ref[...]
                mix_scr[rows, vc] = on.astype(BF16) * z_ref[0, rows, vc]
        for h in range(GLA_HEADS):
            sf_scr[h] = states[h]
        y = _dot(mix_scr[...], wo_ref[...])
        gate = mod_ref[0][2:3, :]
        o_ref[0] = x_ref[0] + gate * y


def _gla_mix(x, mod, q, k, v, z, gf, gb, gw, w_out, tb):
    bsz, s, d = x.shape
    nt = s // tb
    const2 = lambda b, j: (0, 0)
    both = lambda b, j: (b, jnp.where(j < nt, nt - 1 - j, j - nt), 0)
    fwd_only = lambda b, j: (b, jnp.maximum(j - nt, 0), 0)
    return pl.pallas_call(
        functools.partial(_gla_kernel, nt=nt),
        grid=(bsz, 2 * nt),
        in_specs=[
            pl.BlockSpec((1, tb, d), fwd_only),
            pl.BlockSpec((1, 3, d), lambda b, j: (b, 0, 0)),
            pl.BlockSpec((1, tb, GLA_KEY_WIDTH), fwd_only),
            pl.BlockSpec((1, tb, GLA_KEY_WIDTH), both),
            pl.BlockSpec((1, tb, GLA_VAL_WIDTH), both),
            pl.BlockSpec((1, tb, GLA_VAL_WIDTH), fwd_only),
            pl.BlockSpec((1, tb, GLA_KEY_WIDTH), fwd_only),
            pl.BlockSpec((1, tb, GLA_KEY_WIDTH), lambda b, j: (b, jnp.maximum(nt - 1 - j, 0), 0)),
            pl.BlockSpec((2, GLA_CHUNK, GLA_CHUNK), lambda b, j: (0, 0, 0)),
            pl.BlockSpec((1, GLA_DV), const2),
            pl.BlockSpec(w_out.shape, const2),
        ],
        out_specs=pl.BlockSpec((1, tb, d), fwd_only),
        out_shape=jax.ShapeDtypeStruct((bsz, s, d), F32),
        scratch_shapes=[
            pltpu.VMEM((GLA_HEADS, GLA_DK, GLA_DV), F32),
            pltpu.VMEM((GLA_HEADS, GLA_DK, GLA_DV), F32),
            pltpu.VMEM((s // GLA_CHUNK, GLA_HEADS, GLA_DK, GLA_DV), BF16),
            pltpu.VMEM((s, GLA_KEY_WIDTH), F32),
            pltpu.VMEM((GLA_HEADS, 2 * GLA_CHUNK, 2 * GLA_DK), BF16),
            pltpu.VMEM((tb, d), BF16),
        ],
        compiler_params=pltpu.CompilerParams(
            dimension_semantics=("parallel", "arbitrary"), vmem_limit_bytes=VMEM_LIMIT),
        name="gla_mix",
    )(x, mod, q, k, v, z, gf, gb, _cumsum_mats(), gw, w_out)


def kernel(x, c, norm_w, w_ada, b_ada, w_in_a, w_pool, pool_scale, q_norm_w, k_norm_w, attn_sink,
           w_out_a, w_in_c, w_gate_up, b_gate, gla_norm_w, w_out_c):
    bsz, s, d = x.shape
    assert d == D_MODEL and s % 512 == 0
    mod_all = _modulation(c, w_ada, b_ada).reshape(DEPTH, bsz, 3, d)
    bias = _attn_bias()
    band = _pool_band()
    icnt = _pool_inv_count(s)
    zeros_g = jnp.zeros((GLA_GATE_RANK, GLA_KEY_WIDTH), F32)
    for l in range(DEPTH):
        mod = mod_all[l]
        nw = norm_w[l].reshape(1, d)
        if l % 2 == 0:
            i = l // 2
            qwt = jnp.broadcast_to(
                jnp.tile(q_norm_w[i] * (LOG2E * ATT_HEAD_DIM ** -0.5), ATT_HEADS)[:, None],
                (ATT_WIDTH, SUB_ROWS))
            kw2 = jnp.tile(k_norm_w[i], 2).reshape(1, LANES)
            sink = jnp.repeat(LOG2E * attn_sink[i].reshape(ATT_KV_HEADS, ATT_GROUP), BLOCK, axis=1)
            w_in = jnp.concatenate(
                [_pool_fold(w_in_a[i], w_pool[i]), w_in_a[i][:, POOL_WIDTH:].astype(BF16)], axis=1)
            u, zp, qt, k, vt, za = _even_in_proj(x, mod, nw, w_in, qwt, kw2, tm=1024)
            x = _even_mix(x, mod, u, zp, qt, k, vt, za, band, icnt,
                          pool_scale[i].reshape(1, POOL_WIDTH), bias, sink,
                          w_out_a[i].astype(BF16), tq=1024)
        else:
            jdx = l // 2
            w_g = jnp.concatenate([
                jnp.concatenate([w_gate_up[jdx, 0], zeros_g], axis=1),
                jnp.concatenate([zeros_g, w_gate_up[jdx, 1]], axis=1)], axis=0).astype(BF16)
            b_g = b_gate[jdx].reshape(1, 2 * GLA_KEY_WIDTH)
            q, k, v, z, gf, gb = _odd_in_proj(x, mod, nw, w_in_c[jdx].astype(BF16), w_g, b_g, tm=1024)
            x = _gla_mix(x, mod, q, k, v, z, gf, gb, gla_norm_w[jdx].reshape(1, GLA_DV),
                         w_out_c[jdx].astype(BF16), tb=1024)
    return x
```

```python
import functools

import jax
import jax.numpy as jnp
import numpy as np
from jax import lax
from jax.experimental import pallas as pl
from jax.experimental.pallas import tpu as pltpu

F32 = jnp.float32
BF16 = jnp.bfloat16

D_MODEL = 1024
DEPTH = 4
POOL_WINDOWS = (2, 4, 8, 16)
POOL_WIDTH = 512
POOL_GROUP_DIM = 128
ATT_HEADS = 8
ATT_KV_HEADS = 2
ATT_GROUP = ATT_HEADS // ATT_KV_HEADS
ATT_HEAD_DIM = 64
ATT_WIDTH = 512
ATT_KV_WIDTH = 128
WINDOW = 128
BLOCK = 128
GLA_HEADS = 4
GLA_KEY_WIDTH = 512
GLA_VAL_WIDTH = 1024
GLA_DK = 128
GLA_DV = 256
GLA_GATE_RANK = 16
GLA_GATE_NORMALIZER = 16.0
EPS = 1e-6
NEG = -1e30
LOG2E = 1.4426950408889634

LANES = 128
POOL_HALO = 64
GLA_CHUNK = 128
SUB_ROWS = 512
ONES_ROWS = 16
VMEM_LIMIT = 56 * 1024 * 1024


def _dot(a, b):
    return jnp.dot(a, b, preferred_element_type=F32)


def _dot_nt(a, b):
    return lax.dot_general(a, b, (((1,), (1,)), ((), ())), preferred_element_type=F32)


def _dot_tn(a, b):
    return lax.dot_general(a, b, (((0,), (0,)), ((), ())), preferred_element_type=F32)


def _split_bf16(x):
    hi = x.astype(BF16)
    lo = (x - hi.astype(F32)).astype(BF16)
    return hi, lo


def _silu(x):
    h = 0.5 * x
    return h + h * jnp.tanh(h)


def _mod_kernel(c_ref, w_ref, b_ref, o_ref):
    sc = _silu(c_ref[...])
    sc_hi, sc_lo = _split_bf16(sc)
    w_hi, w_lo = _split_bf16(w_ref[0])
    acc = _dot(sc_hi, w_hi) + _dot(sc_lo, w_hi) + _dot(sc_hi, w_lo)
    o_ref[0] = acc + b_ref[0]


def _modulation(c, w_ada, b_ada):
    depth, d, n = w_ada.shape
    bsz = c.shape[0]
    tn = 1024
    return pl.pallas_call(
        _mod_kernel,
        grid=(depth, n // tn),
        in_specs=[
            pl.BlockSpec((bsz, d), lambda l, j: (0, 0)),
            pl.BlockSpec((1, d, tn), lambda l, j: (l, 0, j)),
            pl.BlockSpec((1, 1, tn), lambda l, j: (l, 0, j)),
        ],
        out_specs=pl.BlockSpec((1, bsz, tn), lambda l, j: (l, 0, j)),
        out_shape=jax.ShapeDtypeStruct((depth, bsz, n), F32),
        compiler_params=pltpu.CompilerParams(
            dimension_semantics=("parallel", "parallel"), vmem_limit_bytes=VMEM_LIMIT),
        name="adaln_mod",
    )(c, w_ada, b_ada.reshape(depth, 1, n))


def _pool_fold_kernel(w_ref, wp_ref, o_ref, *, groups):
    g = pl.program_id(0)

    @pl.when(g < groups)
    def _():
        w_hi, w_lo = _split_bf16(w_ref[0])
        p_hi, p_lo = _split_bf16(wp_ref[0, 0])
        o_ref[...] = (_dot(w_hi, p_hi) + _dot(w_lo, p_hi) + _dot(w_hi, p_lo)).astype(BF16)

    @pl.when(g >= groups)
    def _():
        o_ref[...] = w_ref[0].astype(BF16)


def _pool_fold(w_in_all, w_pool_all, layer):
    _, d, n = w_in_all.shape
    _, groups, gd, _ = w_pool_all.shape
    return pl.pallas_call(
        functools.partial(_pool_fold_kernel, groups=groups),
        grid=(n // gd,),
        in_specs=[
            pl.BlockSpec((1, d, gd), lambda g: (layer, 0, g)),
            pl.BlockSpec((1, 1, gd, gd), lambda g: (layer, jnp.minimum(g, groups - 1), 0, 0)),
        ],
        out_specs=pl.BlockSpec((d, gd), lambda g: (0, g)),
        out_shape=jax.ShapeDtypeStruct((d, n), BF16),
        compiler_params=pltpu.CompilerParams(dimension_semantics=("parallel",)),
        name="pool_fold",
    )(w_in_all, w_pool_all)


def _two_stage_pipeline(n, produce, consume):
    always = pl.program_id(0) >= 0
    produce(0)
    for r in range(n):
        @pl.when(always)
        def _(r=r):
            if r + 1 < n:
                produce(r + 1)
            consume(r)


def _norm_modulate(x, nw, mod):
    ms = jnp.mean(x * x, axis=-1, keepdims=True)
    y = (x * lax.rsqrt(ms + EPS)) * nw
    shift = mod[0:1, :]
    scale = mod[1:2, :]
    return (y * (1.0 + scale) + shift).astype(BF16)


def _head_rms(p, w2, head_dim):
    assert 2 * head_dim == LANES and p.shape[-1] == LANES
    lane = lax.broadcasted_iota(jnp.int32, (1, LANES), 1)
    first = lane < head_dim
    sq = p * p
    s0 = jnp.sum(jnp.where(first, sq, 0.0), axis=-1, keepdims=True)
    s1 = jnp.sum(jnp.where(first, 0.0, sq), axis=-1, keepdims=True)
    ms = jnp.where(first, s0, s1) * (1.0 / head_dim)
    return (p * lax.rsqrt(ms + EPS)) * w2


def _even_in_kernel(x_ref, mod_ref, nw_ref, w_ref, qwt_ref, kw_ref,
                    u_ref, zp_ref, qt_ref, k_ref, vt_ref, za_ref):
    c_u, c_zp, c_q = 0, POOL_WIDTH, 2 * POOL_WIDTH
    c_kv = c_q + ATT_WIDTH
    c_za = c_kv + 2 * ATT_KV_WIDTH
    for r in range(x_ref.shape[1] // SUB_ROWS):
        tok = slice(r * SUB_ROWS, (r + 1) * SUB_ROWS)
        hb = _norm_modulate(x_ref[0, tok, :], nw_ref[...], mod_ref[0])
        pqt = _dot(hb, w_ref[:, c_q:c_q + ATT_WIDTH]).T
        for h in range(ATT_HEADS):
            rows = slice(h * ATT_HEAD_DIM, (h + 1) * ATT_HEAD_DIM)
            ph = pqt[rows, :]
            ms = jnp.mean(ph * ph, axis=0, keepdims=True)
            qt_ref[0, rows, tok] = ((ph * lax.rsqrt(ms + EPS)) * qwt_ref[rows, :]).astype(BF16)
        pkv = _dot(hb, w_ref[:, c_kv:c_kv + 2 * ATT_KV_WIDTH])
        k_ref[0, tok, :] = _head_rms(pkv[:, 0:ATT_KV_WIDTH], kw_ref[...], ATT_HEAD_DIM).astype(BF16)
        vt_ref[0, :, tok] = pkv[:, ATT_KV_WIDTH:].T.astype(BF16)
        zp_ref[0, tok, :] = _silu(_dot(hb, w_ref[:, c_zp:c_zp + POOL_WIDTH])).astype(BF16)
        za_ref[0, tok, :] = _silu(_dot(hb, w_ref[:, c_za:c_za + ATT_WIDTH])).astype(BF16)
        u_ref[0, tok, :] = _dot(hb, w_ref[:, c_u:c_u + POOL_WIDTH]).astype(BF16)


def _even_in_proj(x, mod, nw, w_in, qwt, kw2, tm):
    bsz, s, d = x.shape
    n = w_in.shape[1]
    row = lambda b, i: (b, i, 0)
    col = lambda b, i: (b, 0, i)
    const2 = lambda b, i: (0, 0)
    outs = (((s, POOL_WIDTH), (tm, POOL_WIDTH), row), ((s, POOL_WIDTH), (tm, POOL_WIDTH), row),
            ((ATT_WIDTH, s), (ATT_WIDTH, tm), col), ((s, ATT_KV_WIDTH), (tm, ATT_KV_WIDTH), row),
            ((ATT_KV_WIDTH, s), (ATT_KV_WIDTH, tm), col), ((s, ATT_WIDTH), (tm, ATT_WIDTH), row))
    return pl.pallas_call(
        _even_in_kernel,
        grid=(bsz, s // tm),
        in_specs=[
            pl.BlockSpec((1, tm, d), row),
            pl.BlockSpec((1, 3, d), lambda b, i: (b, 0, 0)),
            pl.BlockSpec((1, d), const2),
            pl.BlockSpec((d, n), const2),
            pl.BlockSpec((ATT_WIDTH, SUB_ROWS), const2),
            pl.BlockSpec((1, LANES), const2),
        ],
        out_specs=[pl.BlockSpec((1,) + blk, imap) for _, blk, imap in outs],
        out_shape=[jax.ShapeDtypeStruct((bsz,) + full, BF16) for full, _, _ in outs],
        compiler_params=pltpu.CompilerParams(
            dimension_semantics=("parallel", "parallel"), vmem_limit_bytes=VMEM_LIMIT),
        name="even_in_proj",
    )(x, mod, nw, w_in, qwt, kw2)


def _even_mix_kernel(x_ref, mod_ref, u_ref, up_ref, un_ref, zp_ref, qt_ref, k_ref, kp_ref, kn_ref,
                     vt_ref, vtp_ref, vtn_ref, za_ref, band_ref, icnt_ref, ps_ref, bias_ref, sink_ref,
                     wo_ref, o_ref, ue_scr, ke_scr, vte_scr, st_scr, m_scr, ot_scr, mix_scr):
    tq = x_ref.shape[1]
    i = pl.program_id(1)
    last = pl.num_programs(1) - 1
    nblk = tq // BLOCK
    kspan = 3 * BLOCK

    ue_scr[0:POOL_HALO, :] = jnp.where(i > 0, up_ref[0], jnp.zeros_like(up_ref[0]))
    ue_scr[POOL_HALO:POOL_HALO + tq, :] = u_ref[0]
    ue_scr[POOL_HALO + tq:, :] = jnp.where(i < last, un_ref[0], jnp.zeros_like(un_ref[0]))
    tots = {}
    for j in range(0, nblk, 2):
        for g in range(len(POOL_WINDOWS)):
            cols = slice(g * POOL_GROUP_DIM, (g + 1) * POOL_GROUP_DIM)
            wins = [ue_scr[jj * BLOCK:jj * BLOCK + BLOCK + 2 * POOL_HALO, cols] for jj in (j, j + 1)]
            both = _dot(band_ref[g], jnp.concatenate(wins, axis=1))
            tots[j, g] = both[:, 0:POOL_GROUP_DIM]
            tots[j + 1, g] = both[:, POOL_GROUP_DIM:]
    for j in range(nblk):
        rows = slice(j * BLOCK, (j + 1) * BLOCK)
        for g, w in enumerate(POOL_WINDOWS):
            cols = slice(g * POOL_GROUP_DIM, (g + 1) * POOL_GROUP_DIM)
            inv_cnt = 1.0 / w
            if j == 0:
                inv_cnt = jnp.where(i == 0, icnt_ref[0, :, cols], inv_cnt)
            if j == nblk - 1:
                inv_cnt = jnp.where(i == last, icnt_ref[1, :, cols], inv_cnt)
            y = (tots[j, g] * inv_cnt - u_ref[0, rows, cols].astype(F32)) * ps_ref[:, cols]
            mix_scr[rows, cols] = (y * zp_ref[0, rows, cols].astype(F32)).astype(BF16)

    ke_scr[0:BLOCK, :] = kp_ref[0]
    ke_scr[BLOCK:BLOCK + tq, :] = k_ref[0]
    ke_scr[BLOCK + tq:, :] = kn_ref[0]
    for kh in range(ATT_KV_HEADS):
        hd = slice(kh * ATT_HEAD_DIM, (kh + 1) * ATT_HEAD_DIM)
        vte_scr[kh, 0:ATT_HEAD_DIM, 0:BLOCK] = vtp_ref[0, hd, :]
        vte_scr[kh, 0:ATT_HEAD_DIM, BLOCK:BLOCK + tq] = vt_ref[0, hd, :]
        vte_scr[kh, 0:ATT_HEAD_DIM, BLOCK + tq:] = vtn_ref[0, hd, :]
        vte_scr[kh, ATT_HEAD_DIM:, :] = jnp.ones((ONES_ROWS, tq + 2 * BLOCK), BF16)

    def score_stage(j):
        rows = slice(j * BLOCK, (j + 1) * BLOCK)
        top = jnp.where(i == 0, 3, 0) if j == 0 else 0
        bot = jnp.where(i == last, 3, 2) if j == nblk - 1 else 2
        for kh in range(ATT_KV_HEADS):
            hd = slice(kh * ATT_HEAD_DIM, (kh + 1) * ATT_HEAD_DIM)
            qs = jnp.concatenate(
                [qt_ref[0, (kh * ATT_GROUP + g) * ATT_HEAD_DIM:(kh * ATT_GROUP + g + 1) * ATT_HEAD_DIM, rows]
                 for g in range(ATT_GROUP)], axis=1)
            qk = _dot(ke_scr[j * BLOCK:j * BLOCK + kspan, hd], qs)
            st = jnp.concatenate(
                [qk[kb * BLOCK:(kb + 1) * BLOCK] + bias_ref[kh * 4 + blk]
                 for kb, blk in enumerate((top, 1, bot))], axis=0)
            st_scr[j % 2, kh] = st
            m_scr[j % 2, kh] = jnp.maximum(jnp.max(st, axis=0, keepdims=True), sink_ref[kh:kh + 1, :])

    def value_stage(j):
        rows = slice(j * BLOCK, (j + 1) * BLOCK)
        for kh in range(ATT_KV_HEADS):
            m = m_scr[j % 2, kh]
            p = jnp.exp2(st_scr[j % 2, kh] - m).astype(BF16)
            pv = _dot(vte_scr[kh, :, j * BLOCK:j * BLOCK + kspan], p)
            den = pv[ATT_HEAD_DIM:ATT_HEAD_DIM + 1, :] + jnp.exp2(sink_ref[kh:kh + 1, :] - m)
            ot = pv[0:ATT_HEAD_DIM, :] * (1.0 / den)
            for g in range(ATT_GROUP):
                h = kh * ATT_GROUP + g
                ot_scr[h * ATT_HEAD_DIM:(h + 1) * ATT_HEAD_DIM, :] = ot[:, g * BLOCK:(g + 1) * BLOCK]
        o = ot_scr[...].T
        mix_scr[rows, POOL_WIDTH:] = (o * za_ref[0, rows, :].astype(F32)).astype(BF16)

    _two_stage_pipeline(nblk, score_stage, value_stage)

    @pl.when(pl.program_id(0) >= 0)
    def _():
        y = _dot(mix_scr[...], wo_ref[...])
        gate = mod_ref[0][2:3, :]
        o_ref[0] = x_ref[0] + gate * y


def _even_mix(x, mod, u, zp, qt, k, vt, za, band, icnt, pool_scale, bias, sink, w_out, tq):
    bsz, s, d = x.shape
    nt = s // tq
    row = lambda b, i: (b, i, 0)
    col = lambda b, i: (b, 0, i)
    const2 = lambda b, i: (0, 0)
    const3 = lambda b, i: (0, 0, 0)
    hp = tq // POOL_HALO
    hb = tq // BLOCK
    prev_row = lambda n: (lambda b, i: (b, jnp.maximum(i * n - 1, 0), 0))
    next_row = lambda n, tot: (lambda b, i: (b, jnp.minimum((i + 1) * n, tot - 1), 0))
    return pl.pallas_call(
        _even_mix_kernel,
        grid=(bsz, nt),
        in_specs=[
            pl.BlockSpec((1, tq, d), row),
            pl.BlockSpec((1, 3, d), lambda b, i: (b, 0, 0)),
            pl.BlockSpec((1, tq, POOL_WIDTH), row),
            pl.BlockSpec((1, POOL_HALO, POOL_WIDTH), prev_row(hp)),
            pl.BlockSpec((1, POOL_HALO, POOL_WIDTH), next_row(hp, s // POOL_HALO)),
            pl.BlockSpec((1, tq, POOL_WIDTH), row),
            pl.BlockSpec((1, ATT_WIDTH, tq), col),
            pl.BlockSpec((1, tq, ATT_KV_WIDTH), row),
            pl.BlockSpec((1, BLOCK, ATT_KV_WIDTH), prev_row(hb)),
            pl.BlockSpec((1, BLOCK, ATT_KV_WIDTH), next_row(hb, s // BLOCK)),
            pl.BlockSpec((1, ATT_KV_WIDTH, tq), col),
            pl.BlockSpec((1, ATT_KV_WIDTH, BLOCK), lambda b, i: (b, 0, jnp.maximum(i * hb - 1, 0))),
            pl.BlockSpec((1, ATT_KV_WIDTH, BLOCK), lambda b, i: (b, 0, jnp.minimum((i + 1) * hb, s // BLOCK - 1))),
            pl.BlockSpec((1, tq, ATT_WIDTH), row),
            pl.BlockSpec(band.shape, const3),
            pl.BlockSpec(icnt.shape, const3),
            pl.BlockSpec((1, POOL_WIDTH), const2),
            pl.BlockSpec(bias.shape, const3),
            pl.BlockSpec(sink.shape, const2),
            pl.BlockSpec(w_out.shape, const2),
        ],
        out_specs=pl.BlockSpec((1, tq, d), row),
        out_shape=jax.ShapeDtypeStruct((bsz, s, d), F32),
        scratch_shapes=[
            pltpu.VMEM((tq + 2 * POOL_HALO, POOL_WIDTH), BF16),
            pltpu.VMEM((tq + 2 * BLOCK, ATT_KV_WIDTH), BF16),
            pltpu.VMEM((ATT_KV_HEADS, ATT_HEAD_DIM + ONES_ROWS, tq + 2 * BLOCK), BF16),
            pltpu.VMEM((2, ATT_KV_HEADS, 3 * BLOCK, ATT_GROUP * BLOCK), F32),
            pltpu.VMEM((2, ATT_KV_HEADS, 1, ATT_GROUP * BLOCK), F32),
            pltpu.VMEM((ATT_WIDTH, BLOCK), F32),
            pltpu.VMEM((tq, d), BF16),
        ],
        compiler_params=pltpu.CompilerParams(
            dimension_semantics=("parallel", "parallel"), vmem_limit_bytes=VMEM_LIMIT),
        name="even_mix",
    )(x, mod, u, u, u, zp, qt, k, k, k, vt, vt, vt, za, band, icnt, pool_scale, bias, sink, w_out)


def _attn_bias():
    r = np.arange(BLOCK)[:, None]
    c = np.arange(3 * BLOCK)[None, :]
    dist = np.abs(r + BLOCK - c)
    slopes = 2.0 ** (-8.0 * np.arange(1, ATT_HEADS + 1, dtype=np.float64) / ATT_HEADS)
    bias = -(LOG2E * slopes)[:, None, None] * dist.astype(np.float64)[None]
    bias = np.where((dist <= WINDOW)[None], bias, NEG)
    bias = bias.reshape(ATT_KV_HEADS, ATT_GROUP, BLOCK, 3, BLOCK)
    bias = np.transpose(bias, (0, 3, 4, 1, 2)).reshape(ATT_KV_HEADS, 3, BLOCK, ATT_GROUP * BLOCK)
    masked = np.full((ATT_KV_HEADS, 1, BLOCK, ATT_GROUP * BLOCK), NEG)
    bias = np.concatenate([bias, masked], axis=1).reshape(ATT_KV_HEADS * 4, BLOCK, ATT_GROUP * BLOCK)
    return jnp.asarray(bias, dtype=F32)


def _pool_band():
    r = np.arange(BLOCK)[:, None]
    c = np.arange(BLOCK + 2 * POOL_HALO)[None, :]
    off = c - POOL_HALO - r
    band = np.stack([((off >= -(w // 2)) & (off <= w // 2 - 1)) for w in POOL_WINDOWS])
    return jnp.asarray(band.astype(np.float32), dtype=BF16)


def _pool_inv_count(seq_len):
    t = np.concatenate([np.arange(BLOCK), np.arange(seq_len - BLOCK, seq_len)])[:, None]
    w = np.repeat(np.asarray(POOL_WINDOWS), POOL_GROUP_DIM)[None, :]
    lo = np.clip(t - w // 2, 0, seq_len - 1)
    hi = np.clip(t + w // 2 - 1, 0, seq_len - 1)
    inv = np.float32(1.0) / (hi - lo + 1).astype(np.float32)
    return jnp.asarray(inv.reshape(2, BLOCK, POOL_WIDTH), dtype=F32)


def _log_sigmoid(x):
    return jnp.minimum(x, 0.0) - jnp.log(1.0 + jnp.exp(-jnp.abs(x)))


def _odd_in_kernel(x_ref, mod_ref, nw_ref, w_ref, wg_ref, bg_ref,
                   q_ref, k_ref, v_ref, z_ref, gf_ref, gb_ref):
    c_q, c_k, c_v = 0, GLA_KEY_WIDTH, 2 * GLA_KEY_WIDTH
    c_z = c_v + GLA_VAL_WIDTH
    c_a = c_z + GLA_VAL_WIDTH
    for r in range(x_ref.shape[1] // SUB_ROWS):
        tok = slice(r * SUB_ROWS, (r + 1) * SUB_ROWS)
        hb = _norm_modulate(x_ref[0, tok, :], nw_ref[...], mod_ref[0])
        q_ref[0, tok, :] = (_dot(hb, w_ref[:, c_q:c_q + GLA_KEY_WIDTH]) * (GLA_DK ** -0.5)).astype(BF16)
        k_ref[0, tok, :] = _dot(hb, w_ref[:, c_k:c_k + GLA_KEY_WIDTH]).astype(BF16)
        v_ref[0, tok, :] = _dot(hb, w_ref[:, c_v:c_v + GLA_VAL_WIDTH]).astype(BF16)
        z_ref[0, tok, :] = _silu(_dot(hb, w_ref[:, c_z:c_z + GLA_VAL_WIDTH])).astype(BF16)
        a = _dot(hb, w_ref[:, c_a:c_a + 2 * GLA_GATE_RANK]).astype(BF16)
        logits = _dot(a, wg_ref[...]) + bg_ref[...]
        g = _log_sigmoid(logits) * (LOG2E / GLA_GATE_NORMALIZER)
        gf_ref[0, tok, :] = g[:, 0:GLA_KEY_WIDTH].astype(BF16)
        gb_ref[0, tok, :] = g[:, GLA_KEY_WIDTH:].astype(BF16)


def _odd_in_proj(x, mod, nw, w_in, w_g, b_g, tm):
    bsz, s, d = x.shape
    row = lambda b, i: (b, i, 0)
    const2 = lambda b, i: (0, 0)
    outs = ((GLA_KEY_WIDTH, BF16), (GLA_KEY_WIDTH, BF16), (GLA_VAL_WIDTH, BF16), (GLA_VAL_WIDTH, BF16),
            (GLA_KEY_WIDTH, BF16), (GLA_KEY_WIDTH, BF16))
    return pl.pallas_call(
        _odd_in_kernel,
        grid=(bsz, s // tm),
        in_specs=[
            pl.BlockSpec((1, tm, d), row),
            pl.BlockSpec((1, 3, d), lambda b, i: (b, 0, 0)),
            pl.BlockSpec((1, d), const2),
            pl.BlockSpec(w_in.shape, const2),
            pl.BlockSpec(w_g.shape, const2),
            pl.BlockSpec(b_g.shape, const2),
        ],
        out_specs=[pl.BlockSpec((1, tm, w), row) for w, _ in outs],
        out_shape=[jax.ShapeDtypeStruct((bsz, s, w), dt) for w, dt in outs],
        compiler_params=pltpu.CompilerParams(
            dimension_semantics=("parallel", "parallel"), vmem_limit_bytes=VMEM_LIMIT),
        name="odd_in_proj",
    )(x, mod, nw, w_in, w_g, b_g)


def _cumsum_mats():
    r = np.arange(GLA_CHUNK)[:, None]
    c = np.arange(GLA_CHUNK)[None, :]
    return jnp.asarray(np.stack([c <= r, c >= r]).astype(np.float32), dtype=BF16)


def _chunk_scan(tri, g):
    return _dot(tri, g)


def _decay_cols(row):
    colb = jnp.broadcast_to(row, (GLA_DK, GLA_DK)).T
    return jnp.concatenate([colb] * (GLA_DV // GLA_DK), axis=1)


def _gla_kernel(x_ref, mod_ref, q_ref, k_ref, v_ref, z_ref, gf_ref, gb_ref, tri_ref, gw_ref, wo_ref,
                o_ref, sf_scr, sb_scr, snap_scr, beta_scr, kbd_scr, mix_scr, *, nt):
    tb = x_ref.shape[1]
    nch = tb // GLA_CHUNK
    j = pl.program_id(1)
    c = GLA_CHUNK
    lower, upper = tri_ref[0], tri_ref[1]
    causal = (lax.broadcasted_iota(jnp.int32, (c, c), 1) <= lax.broadcasted_iota(jnp.int32, (c, c), 0))

    @pl.when(j == 0)
    def _():
        sf_scr[...] = jnp.zeros_like(sf_scr)
        sb_scr[...] = jnp.zeros_like(sb_scr)
        kbd_scr[...] = jnp.zeros_like(kbd_scr)

    @pl.when(j < nt)
    def _():
        blk = nt - 1 - j
        betas = [_chunk_scan(upper, gb_ref[0, cc * c:(cc + 1) * c, :]) for cc in range(nch)]
        upd, dec = {}, {}
        for cc in range(nch - 1, -1, -1):
            rows = slice(cc * c, (cc + 1) * c)
            beta_scr[pl.ds((blk * nch + cc) * c, c), :] = betas[cc]
            for h in range(GLA_HEADS):
                kc = slice(h * GLA_DK, (h + 1) * GLA_DK)
                vc = slice(h * GLA_DV, (h + 1) * GLA_DV)
                bh = betas[cc][:, kc]
                first = bh[0:1, :]
                kx = k_ref[0, rows, kc] * jnp.exp2(first - bh).astype(BF16)
                upd[cc, h] = _dot_tn(kx, v_ref[0, rows, vc])
                dec[cc, h] = _decay_cols(jnp.exp2(first))
        for h in range(GLA_HEADS):
            st = sb_scr[h]
            for cc in range(nch - 1, -1, -1):
                snap_scr[blk * nch + cc, h] = st.astype(BF16)
                st = dec[cc, h] * st + upd[cc, h]
            sb_scr[h] = st

    @pl.when(j >= nt)
    def _():
        blk = j - nt
        bfw_next = _chunk_scan(lower, gf_ref[0, 0:c, :])
        states = [sf_scr[h] for h in range(GLA_HEADS)]
        for cc in range(nch):
            rows = slice(cc * c, (cc + 1) * c)
            bfw = bfw_next
            beta = beta_scr[pl.ds((blk * nch + cc) * c, c), :]
            a2 = []
            for h in range(GLA_HEADS):
                kc = slice(h * GLA_DK, (h + 1) * GLA_DK)
                qh = q_ref[0, rows, kc]
                kh = k_ref[0, rows, kc]
                bf = bfw[:, kc]
                bb = beta[:, kc]
                rf = bf[c // 2:c // 2 + 1, :]
                rb = bb[c // 2 - 1:c // 2, :]
                qcat = jnp.concatenate(
                    [qh * jnp.exp2(bf - rf).astype(BF16), qh * jnp.exp2(bb - rb).astype(BF16)], axis=1)
                kbd_scr[h, 0:c, 0:GLA_DK] = kh * jnp.exp2(rf - bf).astype(BF16)
                kbd_scr[h, c:2 * c, GLA_DK:2 * GLA_DK] = kh * jnp.exp2(rb - bb).astype(BF16)
                a2.append(_dot_nt(qcat, kbd_scr[h]))
            upd = []
            for h in range(GLA_HEADS):
                kc = slice(h * GLA_DK, (h + 1) * GLA_DK)
                vc = slice(h * GLA_DV, (h + 1) * GLA_DV)
                bf = bfw[:, kc]
                kx = k_ref[0, rows, kc] * jnp.exp2(bf[c - 1:c, :] - bf).astype(BF16)
                upd.append(_dot_tn(kx, v_ref[0, rows, vc]))
            if cc + 1 < nch:
                bfw_next = _chunk_scan(lower, gf_ref[0, (cc + 1) * c:(cc + 2) * c, :])
            outs = []
            for h in range(GLA_HEADS):
                kc = slice(h * GLA_DK, (h + 1) * GLA_DK)
                vc = slice(h * GLA_DV, (h + 1) * GLA_DV)
                bf = bfw[:, kc]
                bb = beta[:, kc]
                att = jnp.where(causal, a2[h][:, 0:c], a2[h][:, c:2 * c]).astype(BF16)
                qh = q_ref[0, rows, kc]
                lhs = jnp.concatenate(
                    [att, qh * jnp.exp2(bf).astype(BF16), qh * jnp.exp2(bb).astype(BF16)], axis=1)
                rhs = jnp.concatenate(
                    [v_ref[0, rows, vc], states[h].astype(BF16), snap_scr[blk * nch + cc, h]], axis=0)
                outs.append(_dot(lhs, rhs))
                states[h] = _decay_cols(jnp.exp2(bf[c - 1:c, :])) * states[h] + upd[h]
            for h in range(GLA_HEADS):
                vc = slice(h * GLA_DV, (h + 1) * GLA_DV)
                o = outs[h]
                ms = jnp.mean(o * o, axis=-1, keepdims=True)
                on = (o * lax.rsqrt(ms + EPS)) * gw_ref[...]
                mix_scr[rows, vc] = on.astype(BF16) * z_ref[0, rows, vc]
        for h in range(GLA_HEADS):
            sf_scr[h] = states[h]
        y = _dot(mix_scr[...], wo_ref[...])
        gate = mod_ref[0][2:3, :]
        o_ref[0] = x_ref[0] + gate * y


def _gla_mix(x, mod, q, k, v, z, gf, gb, gw, w_out, tb):
    bsz, s, d = x.shape
    nt = s // tb
    const2 = lambda b, j: (0, 0)
    both = lambda b, j: (b, jnp.where(j < nt, nt - 1 - j, j - nt), 0)
    fwd_only = lambda b, j: (b, jnp.maximum(j - nt, 0), 0)
    return pl.pallas_call(
        functools.partial(_gla_kernel, nt=nt),
        grid=(bsz, 2 * nt),
        in_specs=[
            pl.BlockSpec((1, tb, d), fwd_only),
            pl.BlockSpec((1, 3, d), lambda b, j: (b, 0, 0)),
            pl.BlockSpec((1, tb, GLA_KEY_WIDTH), fwd_only),
            pl.BlockSpec((1, tb, GLA_KEY_WIDTH), both),
            pl.BlockSpec((1, tb, GLA_VAL_WIDTH), both),
            pl.BlockSpec((1, tb, GLA_VAL_WIDTH), fwd_only),
            pl.BlockSpec((1, tb, GLA_KEY_WIDTH), fwd_only),
            pl.BlockSpec((1, tb, GLA_KEY_WIDTH), lambda b, j: (b, jnp.maximum(nt - 1 - j, 0), 0)),
            pl.BlockSpec((2, GLA_CHUNK, GLA_CHUNK), lambda b, j: (0, 0, 0)),
            pl.BlockSpec((1, GLA_DV), const2),
            pl.BlockSpec(w_out.shape, const2),
        ],
        out_specs=pl.BlockSpec((1, tb, d), fwd_only),
        out_shape=jax.ShapeDtypeStruct((bsz, s, d), F32),
        scratch_shapes=[
            pltpu.VMEM((GLA_HEADS, GLA_DK, GLA_DV), F32),
            pltpu.VMEM((GLA_HEADS, GLA_DK, GLA_DV), F32),
            pltpu.VMEM((s // GLA_CHUNK, GLA_HEADS, GLA_DK, GLA_DV), BF16),
            pltpu.VMEM((s, GLA_KEY_WIDTH), F32),
            pltpu.VMEM((GLA_HEADS, 2 * GLA_CHUNK, 2 * GLA_DK), BF16),
            pltpu.VMEM((tb, d), BF16),
        ],
        compiler_params=pltpu.CompilerParams(
            dimension_semantics=("parallel", "arbitrary"), vmem_limit_bytes=VMEM_LIMIT),
        name="gla_mix",
    )(x, mod, q, k, v, z, gf, gb, _cumsum_mats(), gw, w_out)


def kernel(x, c, norm_w, w_ada, b_ada, w_in_a, w_pool, pool_scale, q_norm_w, k_norm_w, attn_sink,
           w_out_a, w_in_c, w_gate_up, b_gate, gla_norm_w, w_out_c):
    bsz, s, d = x.shape
    assert d == D_MODEL and s % 512 == 0
    mod_all = _modulation(c, w_ada, b_ada).reshape(DEPTH, bsz, 3, d)
    bias = _attn_bias()
    band = _pool_band()
    icnt = _pool_inv_count(s)
    zeros_g = jnp.zeros((GLA_GATE_RANK, GLA_KEY_WIDTH), F32)
    for l in range(DEPTH):
        mod = mod_all[l]
        nw = norm_w[l].reshape(1, d)
        if l % 2 == 0:
            i = l // 2
            qwt = jnp.broadcast_to(
                jnp.tile(q_norm_w[i] * (LOG2E * ATT_HEAD_DIM ** -0.5), ATT_HEADS)[:, None],
                (ATT_WIDTH, SUB_ROWS))
            kw2 = jnp.tile(k_norm_w[i], 2).reshape(1, LANES)
            sink = jnp.repeat(LOG2E * attn_sink[i].reshape(ATT_KV_HEADS, ATT_GROUP), BLOCK, axis=1)
            u, zp, qt, k, vt, za = _even_in_proj(
                x, mod, nw, _pool_fold(w_in_a, w_pool, i), qwt, kw2, tm=1024)
            x = _even_mix(x, mod, u, zp, qt, k, vt, za, band, icnt,
                          pool_scale[i].reshape(1, POOL_WIDTH), bias, sink,
                          w_out_a[i].astype(BF16), tq=1024)
        else:
            jdx = l // 2
            w_g = jnp.concatenate([
                jnp.concatenate([w_gate_up[jdx, 0], zeros_g], axis=1),
                jnp.concatenate([zeros_g, w_gate_up[jdx, 1]], axis=1)], axis=0).astype(BF16)
            b_g = b_gate[jdx].reshape(1, 2 * GLA_KEY_WIDTH)
            q, k, v, z, gf, gb = _odd_in_proj(x, mod, nw, w_in_c[jdx].astype(BF16), w_g, b_g, tm=1024)
            x = _gla_mix(x, mod, q, k, v, z, gf, gb, gla_norm_w[jdx].reshape(1, GLA_DV),
                         w_out_c[jdx].astype(BF16), tb=1024)
    return x
```

```python
import functools

import jax
import jax.numpy as jnp
import numpy as np
from jax import lax
from jax.experimental import pallas as pl
from jax.experimental.pallas import tpu as pltpu

F32 = jnp.float32
BF16 = jnp.bfloat16

D_MODEL = 1024
DEPTH = 4
POOL_WINDOWS = (2, 4, 8, 16)
POOL_WIDTH = 512
POOL_GROUP_DIM = 128
ATT_HEADS = 8
ATT_KV_HEADS = 2
ATT_GROUP = ATT_HEADS // ATT_KV_HEADS
ATT_HEAD_DIM = 64
ATT_WIDTH = 512
ATT_KV_WIDTH = 128
WINDOW = 128
BLOCK = 128
GLA_HEADS = 4
GLA_KEY_WIDTH = 512
GLA_VAL_WIDTH = 1024
GLA_DK = 128
GLA_DV = 256
GLA_GATE_RANK = 16
GLA_GATE_NORMALIZER = 16.0
EPS = 1e-6
NEG = -1e30
LOG2E = 1.4426950408889634

LANES = 128
POOL_HALO = 64
GLA_CHUNK = 128
SUB_ROWS = 512
ONES_ROWS = 16
VMEM_LIMIT = 56 * 1024 * 1024


def _dot(a, b):
    return jnp.dot(a, b, preferred_element_type=F32)


def _dot_nt(a, b):
    return lax.dot_general(a, b, (((1,), (1,)), ((), ())), preferred_element_type=F32)


def _dot_tn(a, b):
    return lax.dot_general(a, b, (((0,), (0,)), ((), ())), preferred_element_type=F32)


def _split_bf16(x):
    hi = x.astype(BF16)
    lo = (x - hi.astype(F32)).astype(BF16)
    return hi, lo


def _silu(x):
    h = 0.5 * x
    return h + h * jnp.tanh(h)


def _mod_kernel(c_ref, w_ref, b_ref, o_ref):
    sc = _silu(c_ref[...])
    sc_hi, sc_lo = _split_bf16(sc)
    w_hi, w_lo = _split_bf16(w_ref[0])
    acc = _dot(sc_hi, w_hi) + _dot(sc_lo, w_hi) + _dot(sc_hi, w_lo)
    o_ref[0] = acc + b_ref[0]


def _modulation(c, w_ada, b_ada):
    depth, d, n = w_ada.shape
    bsz = c.shape[0]
    tn = 1024
    return pl.pallas_call(
        _mod_kernel,
        grid=(depth, n // tn),
        in_specs=[
            pl.BlockSpec((bsz, d), lambda l, j: (0, 0)),
            pl.BlockSpec((1, d, tn), lambda l, j: (l, 0, j)),
            pl.BlockSpec((1, 1, tn), lambda l, j: (l, 0, j)),
        ],
        out_specs=pl.BlockSpec((1, bsz, tn), lambda l, j: (l, 0, j)),
        out_shape=jax.ShapeDtypeStruct((depth, bsz, n), F32),
        compiler_params=pltpu.CompilerParams(
            dimension_semantics=("parallel", "parallel"), vmem_limit_bytes=VMEM_LIMIT),
        name="adaln_mod",
    )(c, w_ada, b_ada.reshape(depth, 1, n))


def _pool_fold_kernel(w_ref, wp_ref, o_ref, *, groups):
    g = pl.program_id(0)

    @pl.when(g < groups)
    def _():
        w_hi, w_lo = _split_bf16(w_ref[0])
        p_hi, p_lo = _split_bf16(wp_ref[0, 0])
        o_ref[...] = (_dot(w_hi, p_hi) + _dot(w_lo, p_hi) + _dot(w_hi, p_lo)).astype(BF16)

    @pl.when(g >= groups)
    def _():
        o_ref[...] = w_ref[0].astype(BF16)


def _pool_fold(w_in_all, w_pool_all, layer):
    _, d, n = w_in_all.shape
    _, groups, gd, _ = w_pool_all.shape
    return pl.pallas_call(
        functools.partial(_pool_fold_kernel, groups=groups),
        grid=(n // gd,),
        in_specs=[
            pl.BlockSpec((1, d, gd), lambda g: (layer, 0, g)),
            pl.BlockSpec((1, 1, gd, gd), lambda g: (layer, jnp.minimum(g, groups - 1), 0, 0)),
        ],
        out_specs=pl.BlockSpec((d, gd), lambda g: (0, g)),
        out_shape=jax.ShapeDtypeStruct((d, n), BF16),
        compiler_params=pltpu.CompilerParams(dimension_semantics=("parallel",)),
        name="pool_fold",
    )(w_in_all, w_pool_all)


def _two_stage_pipeline(n, produce, consume):
    always = pl.program_id(0) >= 0
    produce(0)
    for r in range(n):
        @pl.when(always)
        def _(r=r):
            if r + 1 < n:
                produce(r + 1)
            consume(r)


def _norm_modulate(x, nw, mod):
    ms = jnp.mean(x * x, axis=-1, keepdims=True)
    y = (x * lax.rsqrt(ms + EPS)) * nw
    shift = mod[0:1, :]
    scale = mod[1:2, :]
    return (y * (1.0 + scale) + shift).astype(BF16)


def _head_rms(p, w2, head_dim):
    assert 2 * head_dim == LANES and p.shape[-1] == LANES
    lane = lax.broadcasted_iota(jnp.int32, (1, LANES), 1)
    first = lane < head_dim
    sq = p * p
    s0 = jnp.sum(jnp.where(first, sq, 0.0), axis=-1, keepdims=True)
    s1 = jnp.sum(jnp.where(first, 0.0, sq), axis=-1, keepdims=True)
    ms = jnp.where(first, s0, s1) * (1.0 / head_dim)
    return (p * lax.rsqrt(ms + EPS)) * w2


def _even_in_kernel(x_ref, mod_ref, nw_ref, w_ref, qwt_ref, kw_ref,
                    u_ref, zp_ref, qt_ref, k_ref, vt_ref, za_ref):
    c_u, c_zp, c_q = 0, POOL_WIDTH, 2 * POOL_WIDTH
    c_kv = c_q + ATT_WIDTH
    c_za = c_kv + 2 * ATT_KV_WIDTH
    for r in range(x_ref.shape[1] // SUB_ROWS):
        tok = slice(r * SUB_ROWS, (r + 1) * SUB_ROWS)
        hb = _norm_modulate(x_ref[0, tok, :], nw_ref[...], mod_ref[0])
        pqt = _dot(hb, w_ref[:, c_q:c_q + ATT_WIDTH]).T
        for h in range(ATT_HEADS):
            rows = slice(h * ATT_HEAD_DIM, (h + 1) * ATT_HEAD_DIM)
            ph = pqt[rows, :]
            ms = jnp.mean(ph * ph, axis=0, keepdims=True)
            qt_ref[0, rows, tok] = ((ph * lax.rsqrt(ms + EPS)) * qwt_ref[rows, :]).astype(BF16)
        pkv = _dot(hb, w_ref[:, c_kv:c_kv + 2 * ATT_KV_WIDTH])
        k_ref[0, tok, :] = _head_rms(pkv[:, 0:ATT_KV_WIDTH], kw_ref[...], ATT_HEAD_DIM).astype(BF16)
        vt_ref[0, :, tok] = pkv[:, ATT_KV_WIDTH:].T.astype(BF16)
        zp_ref[0, tok, :] = _silu(_dot(hb, w_ref[:, c_zp:c_zp + POOL_WIDTH])).astype(BF16)
        za_ref[0, tok, :] = _silu(_dot(hb, w_ref[:, c_za:c_za + ATT_WIDTH])).astype(BF16)
        u_ref[0, tok, :] = _dot(hb, w_ref[:, c_u:c_u + POOL_WIDTH]).astype(BF16)


def _even_in_proj(x, mod, nw, w_in, qwt, kw2, tm):
    bsz, s, d = x.shape
    n = w_in.shape[1]
    row = lambda b, i: (b, i, 0)
    col = lambda b, i: (b, 0, i)
    const2 = lambda b, i: (0, 0)
    outs = (((s, POOL_WIDTH), (tm, POOL_WIDTH), row), ((s, POOL_WIDTH), (tm, POOL_WIDTH), row),
            ((ATT_WIDTH, s), (ATT_WIDTH, tm), col), ((s, ATT_KV_WIDTH), (tm, ATT_KV_WIDTH), row),
            ((ATT_KV_WIDTH, s), (ATT_KV_WIDTH, tm), col), ((s, ATT_WIDTH), (tm, ATT_WIDTH), row))
    return pl.pallas_call(
        _even_in_kernel,
        grid=(bsz, s // tm),
        in_specs=[
            pl.BlockSpec((1, tm, d), row),
            pl.BlockSpec((1, 3, d), lambda b, i: (b, 0, 0)),
            pl.BlockSpec((1, d), const2),
            pl.BlockSpec((d, n), const2),
            pl.BlockSpec((ATT_WIDTH, SUB_ROWS), const2),
            pl.BlockSpec((1, LANES), const2),
        ],
        out_specs=[pl.BlockSpec((1,) + blk, imap) for _, blk, imap in outs],
        out_shape=[jax.ShapeDtypeStruct((bsz,) + full, BF16) for full, _, _ in outs],
        compiler_params=pltpu.CompilerParams(
            dimension_semantics=("parallel", "parallel"), vmem_limit_bytes=VMEM_LIMIT),
        name="even_in_proj",
    )(x, mod, nw, w_in, qwt, kw2)


def _even_mix_kernel(x_ref, mod_ref, u_ref, up_ref, un_ref, zp_ref, qt_ref, k_ref, kp_ref, kn_ref,
                     vt_ref, vtp_ref, vtn_ref, za_ref, band_ref, icnt_ref, ps_ref, bias_ref, sink_ref,
                     wo_ref, o_ref, ue_scr, ke_scr, vte_scr, st_scr, m_scr, ot_scr, mix_scr):
    tq = x_ref.shape[1]
    i = pl.program_id(1)
    last = pl.num_programs(1) - 1
    nblk = tq // BLOCK
    kspan = 3 * BLOCK

    ue_scr[0:POOL_HALO, :] = jnp.where(i > 0, up_ref[0], jnp.zeros_like(up_ref[0]))
    ue_scr[POOL_HALO:POOL_HALO + tq, :] = u_ref[0]
    ue_scr[POOL_HALO + tq:, :] = jnp.where(i < last, un_ref[0], jnp.zeros_like(un_ref[0]))
    tots = {}
    for j in range(0, nblk, 2):
        for g in range(len(POOL_WINDOWS)):
            cols = slice(g * POOL_GROUP_DIM, (g + 1) * POOL_GROUP_DIM)
            wins = [ue_scr[jj * BLOCK:jj * BLOCK + BLOCK + 2 * POOL_HALO, cols] for jj in (j, j + 1)]
            both = _dot(band_ref[g], jnp.concatenate(wins, axis=1))
            tots[j, g] = both[:, 0:POOL_GROUP_DIM]
            tots[j + 1, g] = both[:, POOL_GROUP_DIM:]
    for j in range(nblk):
        rows = slice(j * BLOCK, (j + 1) * BLOCK)
        for g, w in enumerate(POOL_WINDOWS):
            cols = slice(g * POOL_GROUP_DIM, (g + 1) * POOL_GROUP_DIM)
            inv_cnt = 1.0 / w
            if j == 0:
                inv_cnt = jnp.where(i == 0, icnt_ref[0, :, cols], inv_cnt)
            if j == nblk - 1:
                inv_cnt = jnp.where(i == last, icnt_ref[1, :, cols], inv_cnt)
            y = (tots[j, g] * inv_cnt - u_ref[0, rows, cols].astype(F32)) * ps_ref[:, cols]
            mix_scr[rows, cols] = (y * zp_ref[0, rows, cols].astype(F32)).astype(BF16)

    ke_scr[0:BLOCK, :] = kp_ref[0]
    ke_scr[BLOCK:BLOCK + tq, :] = k_ref[0]
    ke_scr[BLOCK + tq:, :] = kn_ref[0]
    for kh in range(ATT_KV_HEADS):
        hd = slice(kh * ATT_HEAD_DIM, (kh + 1) * ATT_HEAD_DIM)
        vte_scr[kh, 0:ATT_HEAD_DIM, 0:BLOCK] = vtp_ref[0, hd, :]
        vte_scr[kh, 0:ATT_HEAD_DIM, BLOCK:BLOCK + tq] = vt_ref[0, hd, :]
        vte_scr[kh, 0:ATT_HEAD_DIM, BLOCK + tq:] = vtn_ref[0, hd, :]
        vte_scr[kh, ATT_HEAD_DIM:, :] = jnp.ones((ONES_ROWS, tq + 2 * BLOCK), BF16)

    def score_stage(j):
        rows = slice(j * BLOCK, (j + 1) * BLOCK)
        top = jnp.where(i == 0, 3, 0) if j == 0 else 0
        bot = jnp.where(i == last, 3, 2) if j == nblk - 1 else 2
        for kh in range(ATT_KV_HEADS):
            hd = slice(kh * ATT_HEAD_DIM, (kh + 1) * ATT_HEAD_DIM)
            qs = jnp.concatenate(
                [qt_ref[0, (kh * ATT_GROUP + g) * ATT_HEAD_DIM:(kh * ATT_GROUP + g + 1) * ATT_HEAD_DIM, rows]
                 for g in range(ATT_GROUP)], axis=1)
            qk = _dot(ke_scr[j * BLOCK:j * BLOCK + kspan, hd], qs)
            st = jnp.concatenate(
                [qk[kb * BLOCK:(kb + 1) * BLOCK] + bias_ref[kh * 4 + blk]
                 for kb, blk in enumerate((top, 1, bot))], axis=0)
            st_scr[j % 2, kh] = st
            m_scr[j % 2, kh] = jnp.maximum(jnp.max(st, axis=0, keepdims=True), sink_ref[kh:kh + 1, :])

    def value_stage(j):
        rows = slice(j * BLOCK, (j + 1) * BLOCK)
        for kh in range(ATT_KV_HEADS):
            m = m_scr[j % 2, kh]
            p = jnp.exp2(st_scr[j % 2, kh] - m).astype(BF16)
            pv = _dot(vte_scr[kh, :, j * BLOCK:j * BLOCK + kspan], p)
            den = pv[ATT_HEAD_DIM:ATT_HEAD_DIM + 1, :] + jnp.exp2(sink_ref[kh:kh + 1, :] - m)
            ot = pv[0:ATT_HEAD_DIM, :] * (1.0 / den)
            for g in range(ATT_GROUP):
                h = kh * ATT_GROUP + g
                ot_scr[h * ATT_HEAD_DIM:(h + 1) * ATT_HEAD_DIM, :] = ot[:, g * BLOCK:(g + 1) * BLOCK]
        o = ot_scr[...].T
        mix_scr[rows, POOL_WIDTH:] = (o * za_ref[0, rows, :].astype(F32)).astype(BF16)

    _two_stage_pipeline(nblk, score_stage, value_stage)

    @pl.when(pl.program_id(0) >= 0)
    def _():
        y = _dot(mix_scr[...], wo_ref[...])
        gate = mod_ref[0][2:3, :]
        o_ref[0] = x_ref[0] + gate * y


def _even_mix(x, mod, u, zp, qt, k, vt, za, band, icnt, pool_scale, bias, sink, w_out, tq):
    bsz, s, d = x.shape
    nt = s // tq
    row = lambda b, i: (b, i, 0)
    col = lambda b, i: (b, 0, i)
    const2 = lambda b, i: (0, 0)
    const3 = lambda b, i: (0, 0, 0)
    hp = tq // POOL_HALO
    hb = tq // BLOCK
    prev_row = lambda n: (lambda b, i: (b, jnp.maximum(i * n - 1, 0), 0))
    next_row = lambda n, tot: (lambda b, i: (b, jnp.minimum((i + 1) * n, tot - 1), 0))
    return pl.pallas_call(
        _even_mix_kernel,
        grid=(bsz, nt),
        in_specs=[
            pl.BlockSpec((1, tq, d), row),
            pl.BlockSpec((1, 3, d), lambda b, i: (b, 0, 0)),
            pl.BlockSpec((1, tq, POOL_WIDTH), row),
            pl.BlockSpec((1, POOL_HALO, POOL_WIDTH), prev_row(hp)),
            pl.BlockSpec((1, POOL_HALO, POOL_WIDTH), next_row(hp, s // POOL_HALO)),
            pl.BlockSpec((1, tq, POOL_WIDTH), row),
            pl.BlockSpec((1, ATT_WIDTH, tq), col),
            pl.BlockSpec((1, tq, ATT_KV_WIDTH), row),
            pl.BlockSpec((1, BLOCK, ATT_KV_WIDTH), prev_row(hb)),
            pl.BlockSpec((1, BLOCK, ATT_KV_WIDTH), next_row(hb, s // BLOCK)),
            pl.BlockSpec((1, ATT_KV_WIDTH, tq), col),
            pl.BlockSpec((1, ATT_KV_WIDTH, BLOCK), lambda b, i: (b, 0, jnp.maximum(i * hb - 1, 0))),
            pl.BlockSpec((1, ATT_KV_WIDTH, BLOCK), lambda b, i: (b, 0, jnp.minimum((i + 1) * hb, s // BLOCK - 1))),
            pl.BlockSpec((1, tq, ATT_WIDTH), row),
            pl.BlockSpec(band.shape, const3),
            pl.BlockSpec(icnt.shape, const3),
            pl.BlockSpec((1, POOL_WIDTH), const2),
            pl.BlockSpec(bias.shape, const3),
            pl.BlockSpec(sink.shape, const2),
            pl.BlockSpec(w_out.shape, const2),
        ],
        out_specs=pl.BlockSpec((1, tq, d), row),
        out_shape=jax.ShapeDtypeStruct((bsz, s, d), F32),
        scratch_shapes=[
            pltpu.VMEM((tq + 2 * POOL_HALO, POOL_WIDTH), BF16),
            pltpu.VMEM((tq + 2 * BLOCK, ATT_KV_WIDTH), BF16),
            pltpu.VMEM((ATT_KV_HEADS, ATT_HEAD_DIM + ONES_ROWS, tq + 2 * BLOCK), BF16),
            pltpu.VMEM((2, ATT_KV_HEADS, 3 * BLOCK, ATT_GROUP * BLOCK), F32),
            pltpu.VMEM((2, ATT_KV_HEADS, 1, ATT_GROUP * BLOCK), F32),
            pltpu.VMEM((ATT_WIDTH, BLOCK), F32),
            pltpu.VMEM((tq, d), BF16),
        ],
        compiler_params=pltpu.CompilerParams(
            dimension_semantics=("parallel", "parallel"), vmem_limit_bytes=VMEM_LIMIT),
        name="even_mix",
    )(x, mod, u, u, u, zp, qt, k, k, k, vt, vt, vt, za, band, icnt, pool_scale, bias, sink, w_out)


def _attn_bias():
    r = np.arange(BLOCK)[:, None]
    c = np.arange(3 * BLOCK)[None, :]
    dist = np.abs(r + BLOCK - c)
    slopes = 2.0 ** (-8.0 * np.arange(1, ATT_HEADS + 1, dtype=np.float64) / ATT_HEADS)
    bias = -(LOG2E * slopes)[:, None, None] * dist.astype(np.float64)[None]
    bias = np.where((dist <= WINDOW)[None], bias, NEG)
    bias = bias.reshape(ATT_KV_HEADS, ATT_GROUP, BLOCK, 3, BLOCK)
    bias = np.transpose(bias, (0, 3, 4, 1, 2)).reshape(ATT_KV_HEADS, 3, BLOCK, ATT_GROUP * BLOCK)
    masked = np.full((ATT_KV_HEADS, 1, BLOCK, ATT_GROUP * BLOCK), NEG)
    bias = np.concatenate([bias, masked], axis=1).reshape(ATT_KV_HEADS * 4, BLOCK, ATT_GROUP * BLOCK)
    return jnp.asarray(bias, dtype=F32)


def _pool_band():
    r = np.arange(BLOCK)[:, None]
    c = np.arange(BLOCK + 2 * POOL_HALO)[None, :]
    off = c - POOL_HALO - r
    band = np.stack([((off >= -(w // 2)) & (off <= w // 2 - 1)) for w in POOL_WINDOWS])
    return jnp.asarray(band.astype(np.float32), dtype=BF16)


def _pool_inv_count(seq_len):
    t = np.concatenate([np.arange(BLOCK), np.arange(seq_len - BLOCK, seq_len)])[:, None]
    w = np.repeat(np.asarray(POOL_WINDOWS), POOL_GROUP_DIM)[None, :]
    lo = np.clip(t - w // 2, 0, seq_len - 1)
    hi = np.clip(t + w // 2 - 1, 0, seq_len - 1)
    inv = np.float32(1.0) / (hi - lo + 1).astype(np.float32)
    return jnp.asarray(inv.reshape(2, BLOCK, POOL_WIDTH), dtype=F32)


def _log_sigmoid(x):
    return jnp.minimum(x, 0.0) - jnp.log(1.0 + jnp.exp(-jnp.abs(x)))


def _odd_in_kernel(x_ref, mod_ref, nw_ref, w_ref, wg_ref, bg_ref,
                   q_ref, k_ref, v_ref, z_ref, gf_ref, gb_ref):
    c_q, c_k, c_v = 0, GLA_KEY_WIDTH, 2 * GLA_KEY_WIDTH
    c_z = c_v + GLA_VAL_WIDTH
    c_a = c_z + GLA_VAL_WIDTH
    for r in range(x_ref.shape[1] // SUB_ROWS):
        tok = slice(r * SUB_ROWS, (r + 1) * SUB_ROWS)
        hb = _norm_modulate(x_ref[0, tok, :], nw_ref[...], mod_ref[0])
        q_ref[0, tok, :] = (_dot(hb, w_ref[:, c_q:c_q + GLA_KEY_WIDTH]) * (GLA_DK ** -0.5)).astype(BF16)
        k_ref[0, tok, :] = _dot(hb, w_ref[:, c_k:c_k + GLA_KEY_WIDTH]).astype(BF16)
        v_ref[0, tok, :] = _dot(hb, w_ref[:, c_v:c_v + GLA_VAL_WIDTH]).astype(BF16)
        z_ref[0, tok, :] = _silu(_dot(hb, w_ref[:, c_z:c_z + GLA_VAL_WIDTH])).astype(BF16)
        a = _dot(hb, w_ref[:, c_a:c_a + 2 * GLA_GATE_RANK]).astype(BF16)
        logits = _dot(a, wg_ref[...]) + bg_ref[...]
        g = _log_sigmoid(logits) * (LOG2E / GLA_GATE_NORMALIZER)
        gf_ref[0, tok, :] = g[:, 0:GLA_KEY_WIDTH].astype(BF16)
        gb_ref[0, tok, :] = g[:, GLA_KEY_WIDTH:].astype(BF16)


def _odd_in_proj(x, mod, nw, w_in, w_g, b_g, tm):
    bsz, s, d = x.shape
    row = lambda b, i: (b, i, 0)
    const2 = lambda b, i: (0, 0)
    outs = ((GLA_KEY_WIDTH, BF16), (GLA_KEY_WIDTH, BF16), (GLA_VAL_WIDTH, BF16), (GLA_VAL_WIDTH, BF16),
            (GLA_KEY_WIDTH, BF16), (GLA_KEY_WIDTH, BF16))
    return pl.pallas_call(
        _odd_in_kernel,
        grid=(bsz, s // tm),
        in_specs=[
            pl.BlockSpec((1, tm, d), row),
            pl.BlockSpec((1, 3, d), lambda b, i: (b, 0, 0)),
            pl.BlockSpec((1, d), const2),
            pl.BlockSpec(w_in.shape, const2),
            pl.BlockSpec(w_g.shape, const2),
            pl.BlockSpec(b_g.shape, const2),
        ],
        out_specs=[pl.BlockSpec((1, tm, w), row) for w, _ in outs],
        out_shape=[jax.ShapeDtypeStruct((bsz, s, w), dt) for w, dt in outs],
        compiler_params=pltpu.CompilerParams(
            dimension_semantics=("parallel", "parallel"), vmem_limit_bytes=VMEM_LIMIT),
        name="odd_in_proj",
    )(x, mod, nw, w_in, w_g, b_g)


def _cumsum_mats():
    r = np.arange(GLA_CHUNK)[:, None]
    c = np.arange(GLA_CHUNK)[None, :]
    return jnp.asarray(np.stack([c <= r, c >= r]).astype(np.float32), dtype=BF16)


def _chunk_scan(tri, g):
    return _dot(tri, g)


def _decay_cols(row):
    colb = jnp.broadcast_to(row, (GLA_DK, GLA_DK)).T
    return jnp.concatenate([colb] * (GLA_DV // GLA_DK), axis=1)


def _gla_kernel(x_ref, mod_ref, q_ref, k_ref, v_ref, z_ref, gf_ref, gb_ref, tri_ref, gw_ref, wo_ref,
                o_ref, sf_scr, sb_scr, snap_scr, beta_scr, kbd_scr, mix_scr, *, nt):
    tb = x_ref.shape[1]
    nch = tb // GLA_CHUNK
    j = pl.program_id(1)
    c = GLA_CHUNK
    lower, upper = tri_ref[0], tri_ref[1]
    row_i = lax.broadcasted_iota(jnp.int32, (c, c), 0)
    col_i = lax.broadcasted_iota(jnp.int32, (c, c), 1)
    half = c // 2
    causal = col_i <= row_i
    same_half = (row_i < half) == (col_i < half)
    top = lax.broadcasted_iota(jnp.int32, (c, 1), 0) < half

    @pl.when(j == 0)
    def _():
        sf_scr[...] = jnp.zeros_like(sf_scr)
        sb_scr[...] = jnp.zeros_like(sb_scr)
        kbd_scr[...] = jnp.zeros_like(kbd_scr)

    @pl.when(j < nt)
    def _():
        blk = nt - 1 - j
        betas = [_chunk_scan(upper, gb_ref[0, cc * c:(cc + 1) * c, :]) for cc in range(nch)]
        upd, dec = {}, {}
        for cc in range(nch - 1, -1, -1):
            rows = slice(cc * c, (cc + 1) * c)
            beta_scr[pl.ds((blk * nch + cc) * c, c), :] = betas[cc]
            for h in range(GLA_HEADS):
                kc = slice(h * GLA_DK, (h + 1) * GLA_DK)
                vc = slice(h * GLA_DV, (h + 1) * GLA_DV)
                bh = betas[cc][:, kc]
                first = bh[0:1, :]
                kx = k_ref[0, rows, kc] * jnp.exp2(first - bh).astype(BF16)
                upd[cc, h] = _dot_tn(kx, v_ref[0, rows, vc])
                dec[cc, h] = _decay_cols(jnp.exp2(first))
        for h in range(GLA_HEADS):
            st = sb_scr[h]
            for cc in range(nch - 1, -1, -1):
                snap_scr[blk * nch + cc, h] = st.astype(BF16)
                st = dec[cc, h] * st + upd[cc, h]
            sb_scr[h] = st

    @pl.when(j >= nt)
    def _():
        blk = j - nt
        bfw_next = _chunk_scan(lower, gf_ref[0, 0:c, :])
        states = [sf_scr[h] for h in range(GLA_HEADS)]
        for cc in range(nch):
            rows = slice(cc * c, (cc + 1) * c)
            bfw = bfw_next
            beta = beta_scr[pl.ds((blk * nch + cc) * c, c), :]
            a2, off = [], []
            for h in range(GLA_HEADS):
                kc = slice(h * GLA_DK, (h + 1) * GLA_DK)
                qh = q_ref[0, rows, kc]
                kh = k_ref[0, rows, kc]
                bf = bfw[:, kc]
                bb = beta[:, kc]
                rf = jnp.where(top, bf[half // 2:half // 2 + 1, :], bf[half + half // 2:half + half // 2 + 1, :])
                rb = jnp.where(top, bb[half // 2 - 1:half // 2, :], bb[half + half // 2 - 1:half + half // 2, :])
                qcat = jnp.concatenate(
                    [qh * jnp.exp2(bf - rf).astype(BF16), qh * jnp.exp2(bb - rb).astype(BF16)], axis=1)
                kbd_scr[h, 0:c, 0:GLA_DK] = kh * jnp.exp2(rf - bf).astype(BF16)
                kbd_scr[h, c:2 * c, GLA_DK:2 * GLA_DK] = kh * jnp.exp2(rb - bb).astype(BF16)
                a2.append(_dot_nt(qcat, kbd_scr[h]))
                sf_row = bf[half - 1:half, :]
                sb_row = bb[half:half + 1, :]
                q_off = qh * jnp.exp2(jnp.where(top, bb - sb_row, bf - sf_row)).astype(BF16)
                k_off = kh * jnp.exp2(jnp.where(top, sf_row - bf, sb_row - bb)).astype(BF16)
                off.append(_dot_nt(q_off, k_off))
            upd = []
            for h in range(GLA_HEADS):
                kc = slice(h * GLA_DK, (h + 1) * GLA_DK)
                vc = slice(h * GLA_DV, (h + 1) * GLA_DV)
                bf = bfw[:, kc]
                kx = k_ref[0, rows, kc] * jnp.exp2(bf[c - 1:c, :] - bf).astype(BF16)
                upd.append(_dot_tn(kx, v_ref[0, rows, vc]))
            if cc + 1 < nch:
                bfw_next = _chunk_scan(lower, gf_ref[0, (cc + 1) * c:(cc + 2) * c, :])
            outs = []
            for h in range(GLA_HEADS):
                kc = slice(h * GLA_DK, (h + 1) * GLA_DK)
                vc = slice(h * GLA_DV, (h + 1) * GLA_DV)
                bf = bfw[:, kc]
                bb = beta[:, kc]
                att = jnp.where(same_half, jnp.where(causal, a2[h][:, 0:c], a2[h][:, c:2 * c]), off[h])
                att = att.astype(BF16)
                qh = q_ref[0, rows, kc]
                lhs = jnp.concatenate(
                    [att, qh * jnp.exp2(bf).astype(BF16), qh * jnp.exp2(bb).astype(BF16)], axis=1)
                rhs = jnp.concatenate(
                    [v_ref[0, rows, vc], states[h].astype(BF16), snap_scr[blk * nch + cc, h]], axis=0)
                outs.append(_dot(lhs, rhs))
                states[h] = _decay_cols(jnp.exp2(bf[c - 1:c, :])) * states[h] + upd[h]
            for h in range(GLA_HEADS):
                vc = slice(h * GLA_DV, (h + 1) * GLA_DV)
                o = outs[h]
                ms = jnp.mean(o * o, axis=-1, keepdims=True)
                on = (o * lax.rsqrt(ms + EPS)) * gw_ref[...]
                mix_scr[rows, vc] = on.astype(BF16) * z_ref[0, rows, vc]
        for h in range(GLA_HEADS):
            sf_scr[h] = states[h]
        y = _dot(mix_scr[...], wo_ref[...])
        gate = mod_ref[0][2:3, :]
        o_ref[0] = x_ref[0] + gate * y


def _gla_mix(x, mod, q, k, v, z, gf, gb, gw, w_out, tb):
    bsz, s, d = x.shape
    nt = s // tb
    const2 = lambda b, j: (0, 0)
    both = lambda b, j: (b, jnp.where(j < nt, nt - 1 - j, j - nt), 0)
    fwd_only = lambda b, j: (b, jnp.maximum(j - nt, 0), 0)
    return pl.pallas_call(
        functools.partial(_gla_kernel, nt=nt),
        grid=(bsz, 2 * nt),
        in_specs=[
            pl.BlockSpec((1, tb, d), fwd_only),
            pl.BlockSpec((1, 3, d), lambda b, j: (b, 0, 0)),
            pl.BlockSpec((1, tb, GLA_KEY_WIDTH), fwd_only),
            pl.BlockSpec((1, tb, GLA_KEY_WIDTH), both),
            pl.BlockSpec((1, tb, GLA_VAL_WIDTH), both),
            pl.BlockSpec((1, tb, GLA_VAL_WIDTH), fwd_only),
            pl.BlockSpec((1, tb, GLA_KEY_WIDTH), fwd_only),
            pl.BlockSpec((1, tb, GLA_KEY_WIDTH), lambda b, j: (b, jnp.maximum(nt - 1 - j, 0), 0)),
            pl.BlockSpec((2, GLA_CHUNK, GLA_CHUNK), lambda b, j: (0, 0, 0)),
            pl.BlockSpec((1, GLA_DV), const2),
            pl.BlockSpec(w_out.shape, const2),
        ],
        out_specs=pl.BlockSpec((1, tb, d), fwd_only),
        out_shape=jax.ShapeDtypeStruct((bsz, s, d), F32),
        scratch_shapes=[
            pltpu.VMEM((GLA_HEADS, GLA_DK, GLA_DV), F32),
            pltpu.VMEM((GLA_HEADS, GLA_DK, GLA_DV), F32),
            pltpu.VMEM((s // GLA_CHUNK, GLA_HEADS, GLA_DK, GLA_DV), BF16),
            pltpu.VMEM((s, GLA_KEY_WIDTH), F32),
            pltpu.VMEM((GLA_HEADS, 2 * GLA_CHUNK, 2 * GLA_DK), BF16),
            pltpu.VMEM((tb, d), BF16),
        ],
        compiler_params=pltpu.CompilerParams(
            dimension_semantics=("parallel", "arbitrary"), vmem_limit_bytes=VMEM_LIMIT),
        name="gla_mix",
    )(x, mod, q, k, v, z, gf, gb, _cumsum_mats(), gw, w_out)


def kernel(x, c, norm_w, w_ada, b_ada, w_in_a, w_pool, pool_scale, q_norm_w, k_norm_w, attn_sink,
           w_out_a, w_in_c, w_gate_up, b_gate, gla_norm_w, w_out_c):
    bsz, s, d = x.shape
    assert d == D_MODEL and s % 512 == 0
    mod_all = _modulation(c, w_ada, b_ada).reshape(DEPTH, bsz, 3, d)
    bias = _attn_bias()
    band = _pool_band()
    icnt = _pool_inv_count(s)
    zeros_g = jnp.zeros((GLA_GATE_RANK, GLA_KEY_WIDTH), F32)
    for l in range(DEPTH):
        mod = mod_all[l]
        nw = norm_w[l].reshape(1, d)
        if l % 2 == 0:
            i = l // 2
            qwt = jnp.broadcast_to(
                jnp.tile(q_norm_w[i] * (LOG2E * ATT_HEAD_DIM ** -0.5), ATT_HEADS)[:, None],
                (ATT_WIDTH, SUB_ROWS))
            kw2 = jnp.tile(k_norm_w[i], 2).reshape(1, LANES)
            sink = jnp.repeat(LOG2E * attn_sink[i].reshape(ATT_KV_HEADS, ATT_GROUP), BLOCK, axis=1)
            u, zp, qt, k, vt, za = _even_in_proj(
                x, mod, nw, _pool_fold(w_in_a, w_pool, i), qwt, kw2, tm=1024)
            x = _even_mix(x, mod, u, zp, qt, k, vt, za, band, icnt,
                          pool_scale[i].reshape(1, POOL_WIDTH), bias, sink,
                          w_out_a[i].astype(BF16), tq=1024)
        else:
            jdx = l // 2
            w_g = jnp.concatenate([
                jnp.concatenate([w_gate_up[jdx, 0], zeros_g], axis=1),
                jnp.concatenate([zeros_g, w_gate_up[jdx, 1]], axis=1)], axis=0).astype(BF16)
            b_g = b_gate[jdx].reshape(1, 2 * GLA_KEY_WIDTH)
            q, k, v, z, gf, gb = _odd_in_proj(x, mod, nw, w_in_c[jdx].astype(BF16), w_g, b_g, tm=1024)
            x = _gla_mix(x, mod, q, k, v, z, gf, gb, gla_norm_w[jdx].reshape(1, GLA_DV),
                         w_out_c[jdx].astype(BF16), tb=1024)
    return x
```

```python
import functools

import jax
import jax.numpy as jnp
import numpy as np
from jax import lax
from jax.experimental import pallas as pl
from jax.experimental.pallas import tpu as pltpu

F32 = jnp.float32
BF16 = jnp.bfloat16

D_MODEL = 1024
DEPTH = 4
POOL_WINDOWS = (2, 4, 8, 16)
POOL_WIDTH = 512
POOL_GROUP_DIM = 128
ATT_HEADS = 8
ATT_KV_HEADS = 2
ATT_GROUP = ATT_HEADS // ATT_KV_HEADS
ATT_HEAD_DIM = 64
ATT_WIDTH = 512
ATT_KV_WIDTH = 128
WINDOW = 128
BLOCK = 128
GLA_HEADS = 4
GLA_KEY_WIDTH = 512
GLA_VAL_WIDTH = 1024
GLA_DK = 128
GLA_DV = 256
GLA_GATE_RANK = 16
GLA_GATE_NORMALIZER = 16.0
EPS = 1e-6
NEG = -1e30
LOG2E = 1.4426950408889634

LANES = 128
POOL_HALO = 64
GLA_CHUNK = 128
SUB_ROWS = 512
ONES_ROWS = 16
VMEM_LIMIT = 56 * 1024 * 1024


def _dot(a, b):
    return jnp.dot(a, b, preferred_element_type=F32)


def _dot_nt(a, b):
    return lax.dot_general(a, b, (((1,), (1,)), ((), ())), preferred_element_type=F32)


def _dot_tn(a, b):
    return lax.dot_general(a, b, (((0,), (0,)), ((), ())), preferred_element_type=F32)


def _split_bf16(x):
    hi = x.astype(BF16)
    lo = (x - hi.astype(F32)).astype(BF16)
    return hi, lo


def _silu(x):
    h = 0.5 * x
    return h + h * jnp.tanh(h)


def _mod_kernel(c_ref, w_ref, b_ref, o_ref):
    sc = _silu(c_ref[...])
    sc_hi, sc_lo = _split_bf16(sc)
    w_hi, w_lo = _split_bf16(w_ref[0])
    acc = _dot(sc_hi, w_hi) + _dot(sc_lo, w_hi) + _dot(sc_hi, w_lo)
    o_ref[0] = acc + b_ref[0]


def _modulation(c, w_ada, b_ada):
    depth, d, n = w_ada.shape
    bsz = c.shape[0]
    tn = 1024
    return pl.pallas_call(
        _mod_kernel,
        grid=(depth, n // tn),
        in_specs=[
            pl.BlockSpec((bsz, d), lambda l, j: (0, 0)),
            pl.BlockSpec((1, d, tn), lambda l, j: (l, 0, j)),
            pl.BlockSpec((1, 1, tn), lambda l, j: (l, 0, j)),
        ],
        out_specs=pl.BlockSpec((1, bsz, tn), lambda l, j: (l, 0, j)),
        out_shape=jax.ShapeDtypeStruct((depth, bsz, n), F32),
        compiler_params=pltpu.CompilerParams(
            dimension_semantics=("parallel", "parallel"), vmem_limit_bytes=VMEM_LIMIT),
        name="adaln_mod",
    )(c, w_ada, b_ada.reshape(depth, 1, n))


def _pool_fold_kernel(w_ref, wp_ref, o_ref, *, groups):
    g = pl.program_id(0)

    @pl.when(g < groups)
    def _():
        w_hi, w_lo = _split_bf16(w_ref[0])
        p_hi, p_lo = _split_bf16(wp_ref[0, 0])
        o_ref[...] = (_dot(w_hi, p_hi) + _dot(w_lo, p_hi) + _dot(w_hi, p_lo)).astype(BF16)

    @pl.when(g >= groups)
    def _():
        o_ref[...] = w_ref[0].astype(BF16)


def _pool_fold(w_in_all, w_pool_all, layer):
    _, d, n = w_in_all.shape
    _, groups, gd, _ = w_pool_all.shape
    return pl.pallas_call(
        functools.partial(_pool_fold_kernel, groups=groups),
        grid=(n // gd,),
        in_specs=[
            pl.BlockSpec((1, d, gd), lambda g: (layer, 0, g)),
            pl.BlockSpec((1, 1, gd, gd), lambda g: (layer, jnp.minimum(g, groups - 1), 0, 0)),
        ],
        out_specs=pl.BlockSpec((d, gd), lambda g: (0, g)),
        out_shape=jax.ShapeDtypeStruct((d, n), BF16),
        compiler_params=pltpu.CompilerParams(dimension_semantics=("parallel",)),
        name="pool_fold",
    )(w_in_all, w_pool_all)


def _two_stage_pipeline(n, produce, consume):
    always = pl.program_id(0) >= 0
    produce(0)
    for r in range(n):
        @pl.when(always)
        def _(r=r):
            if r + 1 < n:
                produce(r + 1)
            consume(r)


def _norm_modulate(x, nw, mod):
    ms = jnp.mean(x * x, axis=-1, keepdims=True)
    y = (x * lax.rsqrt(ms + EPS)) * nw
    shift = mod[0:1, :]
    scale = mod[1:2, :]
    return (y * (1.0 + scale) + shift).astype(BF16)


def _head_rms(p, w2, head_dim):
    assert 2 * head_dim == LANES and p.shape[-1] == LANES
    lane = lax.broadcasted_iota(jnp.int32, (1, LANES), 1)
    first = lane < head_dim
    sq = p * p
    s0 = jnp.sum(jnp.where(first, sq, 0.0), axis=-1, keepdims=True)
    s1 = jnp.sum(jnp.where(first, 0.0, sq), axis=-1, keepdims=True)
    ms = jnp.where(first, s0, s1) * (1.0 / head_dim)
    return (p * lax.rsqrt(ms + EPS)) * w2


def _even_in_kernel(x_ref, mod_ref, nw_ref, w_ref, qwt_ref, kw_ref,
                    u_ref, zp_ref, qt_ref, k_ref, vt_ref, za_ref):
    c_u, c_zp, c_q = 0, POOL_WIDTH, 2 * POOL_WIDTH
    c_kv = c_q + ATT_WIDTH
    c_za = c_kv + 2 * ATT_KV_WIDTH
    for r in range(x_ref.shape[1] // SUB_ROWS):
        tok = slice(r * SUB_ROWS, (r + 1) * SUB_ROWS)
        hb = _norm_modulate(x_ref[0, tok, :], nw_ref[...], mod_ref[0])
        pqt = _dot(hb, w_ref[:, c_q:c_q + ATT_WIDTH]).T
        for h in range(ATT_HEADS):
            rows = slice(h * ATT_HEAD_DIM, (h + 1) * ATT_HEAD_DIM)
            ph = pqt[rows, :]
            ms = jnp.mean(ph * ph, axis=0, keepdims=True)
            qt_ref[0, rows, tok] = ((ph * lax.rsqrt(ms + EPS)) * qwt_ref[rows, :]).astype(BF16)
        pkv = _dot(hb, w_ref[:, c_kv:c_kv + 2 * ATT_KV_WIDTH])
        k_ref[0, tok, :] = _head_rms(pkv[:, 0:ATT_KV_WIDTH], kw_ref[...], ATT_HEAD_DIM).astype(BF16)
        vt_ref[0, :, tok] = pkv[:, ATT_KV_WIDTH:].T.astype(BF16)
        zp_ref[0, tok, :] = _silu(_dot(hb, w_ref[:, c_zp:c_zp + POOL_WIDTH])).astype(BF16)
        za_ref[0, tok, :] = _silu(_dot(hb, w_ref[:, c_za:c_za + ATT_WIDTH])).astype(BF16)
        u_ref[0, tok, :] = _dot(hb, w_ref[:, c_u:c_u + POOL_WIDTH]).astype(BF16)


def _even_in_proj(x, mod, nw, w_in, qwt, kw2, tm):
    bsz, s, d = x.shape
    n = w_in.shape[1]
    row = lambda b, i: (b, i, 0)
    col = lambda b, i: (b, 0, i)
    const2 = lambda b, i: (0, 0)
    outs = (((s, POOL_WIDTH), (tm, POOL_WIDTH), row), ((s, POOL_WIDTH), (tm, POOL_WIDTH), row),
            ((ATT_WIDTH, s), (ATT_WIDTH, tm), col), ((s, ATT_KV_WIDTH), (tm, ATT_KV_WIDTH), row),
            ((ATT_KV_WIDTH, s), (ATT_KV_WIDTH, tm), col), ((s, ATT_WIDTH), (tm, ATT_WIDTH), row))
    return pl.pallas_call(
        _even_in_kernel,
        grid=(bsz, s // tm),
        in_specs=[
            pl.BlockSpec((1, tm, d), row),
            pl.BlockSpec((1, 3, d), lambda b, i: (b, 0, 0)),
            pl.BlockSpec((1, d), const2),
            pl.BlockSpec((d, n), const2),
            pl.BlockSpec((ATT_WIDTH, SUB_ROWS), const2),
            pl.BlockSpec((1, LANES), const2),
        ],
        out_specs=[pl.BlockSpec((1,) + blk, imap) for _, blk, imap in outs],
        out_shape=[jax.ShapeDtypeStruct((bsz,) + full, BF16) for full, _, _ in outs],
        compiler_params=pltpu.CompilerParams(
            dimension_semantics=("parallel", "parallel"), vmem_limit_bytes=VMEM_LIMIT),
        name="even_in_proj",
    )(x, mod, nw, w_in, qwt, kw2)


def _even_mix_kernel(x_ref, mod_ref, u_ref, up_ref, un_ref, zp_ref, qt_ref, k_ref, kp_ref, kn_ref,
                     vt_ref, vtp_ref, vtn_ref, za_ref, band_ref, icnt_ref, ps_ref, bias_ref, sink_ref,
                     wo_ref, o_ref, ue_scr, ke_scr, vte_scr, st_scr, m_scr, ot_scr, mix_scr):
    tq = x_ref.shape[1]
    i = pl.program_id(1)
    last = pl.num_programs(1) - 1
    nblk = tq // BLOCK
    kspan = 3 * BLOCK

    ue_scr[0:POOL_HALO, :] = jnp.where(i > 0, up_ref[0], jnp.zeros_like(up_ref[0]))
    ue_scr[POOL_HALO:POOL_HALO + tq, :] = u_ref[0]
    ue_scr[POOL_HALO + tq:, :] = jnp.where(i < last, un_ref[0], jnp.zeros_like(un_ref[0]))
    tots = {}
    for j in range(0, nblk, 2):
        for g in range(len(POOL_WINDOWS)):
            cols = slice(g * POOL_GROUP_DIM, (g + 1) * POOL_GROUP_DIM)
            wins = [ue_scr[jj * BLOCK:jj * BLOCK + BLOCK + 2 * POOL_HALO, cols] for jj in (j, j + 1)]
            both = _dot(band_ref[g], jnp.concatenate(wins, axis=1))
            tots[j, g] = both[:, 0:POOL_GROUP_DIM]
            tots[j + 1, g] = both[:, POOL_GROUP_DIM:]
    for j in range(nblk):
        rows = slice(j * BLOCK, (j + 1) * BLOCK)
        for g, w in enumerate(POOL_WINDOWS):
            cols = slice(g * POOL_GROUP_DIM, (g + 1) * POOL_GROUP_DIM)
            inv_cnt = 1.0 / w
            if j == 0:
                inv_cnt = jnp.where(i == 0, icnt_ref[0, :, cols], inv_cnt)
            if j == nblk - 1:
                inv_cnt = jnp.where(i == last, icnt_ref[1, :, cols], inv_cnt)
            y = (tots[j, g] * inv_cnt - u_ref[0, rows, cols].astype(F32)) * ps_ref[:, cols]
            mix_scr[rows, cols] = (y * zp_ref[0, rows, cols].astype(F32)).astype(BF16)

    ke_scr[0:BLOCK, :] = kp_ref[0]
    ke_scr[BLOCK:BLOCK + tq, :] = k_ref[0]
    ke_scr[BLOCK + tq:, :] = kn_ref[0]
    for kh in range(ATT_KV_HEADS):
        hd = slice(kh * ATT_HEAD_DIM, (kh + 1) * ATT_HEAD_DIM)
        vte_scr[kh, 0:ATT_HEAD_DIM, 0:BLOCK] = vtp_ref[0, hd, :]
        vte_scr[kh, 0:ATT_HEAD_DIM, BLOCK:BLOCK + tq] = vt_ref[0, hd, :]
        vte_scr[kh, 0:ATT_HEAD_DIM, BLOCK + tq:] = vtn_ref[0, hd, :]
        vte_scr[kh, ATT_HEAD_DIM:, :] = jnp.ones((ONES_ROWS, tq + 2 * BLOCK), BF16)

    def score_stage(j):
        rows = slice(j * BLOCK, (j + 1) * BLOCK)
        top = jnp.where(i == 0, 3, 0) if j == 0 else 0
        bot = jnp.where(i == last, 3, 2) if j == nblk - 1 else 2
        for kh in range(ATT_KV_HEADS):
            hd = slice(kh * ATT_HEAD_DIM, (kh + 1) * ATT_HEAD_DIM)
            qs = jnp.concatenate(
                [qt_ref[0, (kh * ATT_GROUP + g) * ATT_HEAD_DIM:(kh * ATT_GROUP + g + 1) * ATT_HEAD_DIM, rows]
                 for g in range(ATT_GROUP)], axis=1)
            qk = _dot(ke_scr[j * BLOCK:j * BLOCK + kspan, hd], qs)
            st = jnp.concatenate(
                [qk[kb * BLOCK:(kb + 1) * BLOCK] + bias_ref[kh * 4 + blk]
                 for kb, blk in enumerate((top, 1, bot))], axis=0)
            st_scr[j % 2, kh] = st
            m_scr[j % 2, kh] = jnp.maximum(jnp.max(st, axis=0, keepdims=True), sink_ref[kh:kh + 1, :])

    def value_stage(j):
        rows = slice(j * BLOCK, (j + 1) * BLOCK)
        for kh in range(ATT_KV_HEADS):
            m = m_scr[j % 2, kh]
            p = jnp.exp2(st_scr[j % 2, kh] - m).astype(BF16)
            pv = _dot(vte_scr[kh, :, j * BLOCK:j * BLOCK + kspan], p)
            den = pv[ATT_HEAD_DIM:ATT_HEAD_DIM + 1, :] + jnp.exp2(sink_ref[kh:kh + 1, :] - m)
            ot = pv[0:ATT_HEAD_DIM, :] * (1.0 / den)
            for g in range(ATT_GROUP):
                h = kh * ATT_GROUP + g
                ot_scr[h * ATT_HEAD_DIM:(h + 1) * ATT_HEAD_DIM, :] = ot[:, g * BLOCK:(g + 1) * BLOCK]
        o = ot_scr[...].T
        mix_scr[rows, POOL_WIDTH:] = (o * za_ref[0, rows, :].astype(F32)).astype(BF16)

    _two_stage_pipeline(nblk, score_stage, value_stage)

    @pl.when(pl.program_id(0) >= 0)
    def _():
        y = _dot(mix_scr[...], wo_ref[...])
        gate = mod_ref[0][2:3, :]
        o_ref[0] = x_ref[0] + gate * y


def _even_mix(x, mod, u, zp, qt, k, vt, za, band, icnt, pool_scale, bias, sink, w_out, tq):
    bsz, s, d = x.shape
    nt = s // tq
    row = lambda b, i: (b, i, 0)
    col = lambda b, i: (b, 0, i)
    const2 = lambda b, i: (0, 0)
    const3 = lambda b, i: (0, 0, 0)
    hp = tq // POOL_HALO
    hb = tq // BLOCK
    prev_row = lambda n: (lambda b, i: (b, jnp.maximum(i * n - 1, 0), 0))
    next_row = lambda n, tot: (lambda b, i: (b, jnp.minimum((i + 1) * n, tot - 1), 0))
    return pl.pallas_call(
        _even_mix_kernel,
        grid=(bsz, nt),
        in_specs=[
            pl.BlockSpec((1, tq, d), row),
            pl.BlockSpec((1, 3, d), lambda b, i: (b, 0, 0)),
            pl.BlockSpec((1, tq, POOL_WIDTH), row),
            pl.BlockSpec((1, POOL_HALO, POOL_WIDTH), prev_row(hp)),
            pl.BlockSpec((1, POOL_HALO, POOL_WIDTH), next_row(hp, s // POOL_HALO)),
            pl.BlockSpec((1, tq, POOL_WIDTH), row),
            pl.BlockSpec((1, ATT_WIDTH, tq), col),
            pl.BlockSpec((1, tq, ATT_KV_WIDTH), row),
            pl.BlockSpec((1, BLOCK, ATT_KV_WIDTH), prev_row(hb)),
            pl.BlockSpec((1, BLOCK, ATT_KV_WIDTH), next_row(hb, s // BLOCK)),
            pl.BlockSpec((1, ATT_KV_WIDTH, tq), col),
            pl.BlockSpec((1, ATT_KV_WIDTH, BLOCK), lambda b, i: (b, 0, jnp.maximum(i * hb - 1, 0))),
            pl.BlockSpec((1, ATT_KV_WIDTH, BLOCK), lambda b, i: (b, 0, jnp.minimum((i + 1) * hb, s // BLOCK - 1))),
            pl.BlockSpec((1, tq, ATT_WIDTH), row),
            pl.BlockSpec(band.shape, const3),
            pl.BlockSpec(icnt.shape, const3),
            pl.BlockSpec((1, POOL_WIDTH), const2),
            pl.BlockSpec(bias.shape, const3),
            pl.BlockSpec(sink.shape, const2),
            pl.BlockSpec(w_out.shape, const2),
        ],
        out_specs=pl.BlockSpec((1, tq, d), row),
        out_shape=jax.ShapeDtypeStruct((bsz, s, d), F32),
        scratch_shapes=[
            pltpu.VMEM((tq + 2 * POOL_HALO, POOL_WIDTH), BF16),
            pltpu.VMEM((tq + 2 * BLOCK, ATT_KV_WIDTH), BF16),
            pltpu.VMEM((ATT_KV_HEADS, ATT_HEAD_DIM + ONES_ROWS, tq + 2 * BLOCK), BF16),
            pltpu.VMEM((2, ATT_KV_HEADS, 3 * BLOCK, ATT_GROUP * BLOCK), F32),
            pltpu.VMEM((2, ATT_KV_HEADS, 1, ATT_GROUP * BLOCK), F32),
            pltpu.VMEM((ATT_WIDTH, BLOCK), F32),
            pltpu.VMEM((tq, d), BF16),
        ],
        compiler_params=pltpu.CompilerParams(
            dimension_semantics=("parallel", "parallel"), vmem_limit_bytes=VMEM_LIMIT),
        name="even_mix",
    )(x, mod, u, u, u, zp, qt, k, k, k, vt, vt, vt, za, band, icnt, pool_scale, bias, sink, w_out)


def _attn_bias():
    r = np.arange(BLOCK)[:, None]
    c = np.arange(3 * BLOCK)[None, :]
    dist = np.abs(r + BLOCK - c)
    slopes = 2.0 ** (-8.0 * np.arange(1, ATT_HEADS + 1, dtype=np.float64) / ATT_HEADS)
    bias = -(LOG2E * slopes)[:, None, None] * dist.astype(np.float64)[None]
    bias = np.where((dist <= WINDOW)[None], bias, NEG)
    bias = bias.reshape(ATT_KV_HEADS, ATT_GROUP, BLOCK, 3, BLOCK)
    bias = np.transpose(bias, (0, 3, 4, 1, 2)).reshape(ATT_KV_HEADS, 3, BLOCK, ATT_GROUP * BLOCK)
    masked = np.full((ATT_KV_HEADS, 1, BLOCK, ATT_GROUP * BLOCK), NEG)
    bias = np.concatenate([bias, masked], axis=1).reshape(ATT_KV_HEADS * 4, BLOCK, ATT_GROUP * BLOCK)
    return jnp.asarray(bias, dtype=F32)


def _pool_band():
    r = np.arange(BLOCK)[:, None]
    c = np.arange(BLOCK + 2 * POOL_HALO)[None, :]
    off = c - POOL_HALO - r
    band = np.stack([((off >= -(w // 2)) & (off <= w // 2 - 1)) for w in POOL_WINDOWS])
    return jnp.asarray(band.astype(np.float32), dtype=BF16)


def _pool_inv_count(seq_len):
    t = np.concatenate([np.arange(BLOCK), np.arange(seq_len - BLOCK, seq_len)])[:, None]
    w = np.repeat(np.asarray(POOL_WINDOWS), POOL_GROUP_DIM)[None, :]
    lo = np.clip(t - w // 2, 0, seq_len - 1)
    hi = np.clip(t + w // 2 - 1, 0, seq_len - 1)
    inv = np.float32(1.0) / (hi - lo + 1).astype(np.float32)
    return jnp.asarray(inv.reshape(2, BLOCK, POOL_WIDTH), dtype=F32)


def _log2_sigmoid(y):
    sign_bit = jnp.uint32(1 << 31)
    neg_abs = lax.bitcast_convert_type(lax.bitcast_convert_type(y, jnp.uint32) | sign_bit, F32)
    return jnp.minimum(y, 0.0) - jnp.log2(1.0 + jnp.exp2(neg_abs))


def _odd_in_kernel(x_ref, mod_ref, nw_ref, w_ref, wg_ref, bg_ref,
                   q_ref, k_ref, v_ref, z_ref, gf_ref, gb_ref):
    c_q, c_k, c_v = 0, GLA_KEY_WIDTH, 2 * GLA_KEY_WIDTH
    c_z = c_v + GLA_VAL_WIDTH
    c_a = c_z + GLA_VAL_WIDTH
    for r in range(x_ref.shape[1] // SUB_ROWS):
        tok = slice(r * SUB_ROWS, (r + 1) * SUB_ROWS)
        hb = _norm_modulate(x_ref[0, tok, :], nw_ref[...], mod_ref[0])
        q_ref[0, tok, :] = (_dot(hb, w_ref[:, c_q:c_q + GLA_KEY_WIDTH]) * (GLA_DK ** -0.5)).astype(BF16)
        k_ref[0, tok, :] = _dot(hb, w_ref[:, c_k:c_k + GLA_KEY_WIDTH]).astype(BF16)
        v_ref[0, tok, :] = _dot(hb, w_ref[:, c_v:c_v + GLA_VAL_WIDTH]).astype(BF16)
        z_ref[0, tok, :] = _silu(_dot(hb, w_ref[:, c_z:c_z + GLA_VAL_WIDTH])).astype(BF16)
        a = _dot(hb, w_ref[:, c_a:c_a + 2 * GLA_GATE_RANK]).astype(BF16)
        logits = _dot(a, wg_ref[...]) + bg_ref[...]
        g = _log2_sigmoid(logits) * (1.0 / GLA_GATE_NORMALIZER)
        gf_ref[0, tok, :] = g[:, 0:GLA_KEY_WIDTH].astype(BF16)
        gb_ref[0, tok, :] = g[:, GLA_KEY_WIDTH:].astype(BF16)


def _odd_in_proj(x, mod, nw, w_in, w_g, b_g, tm):
    bsz, s, d = x.shape
    row = lambda b, i: (b, i, 0)
    const2 = lambda b, i: (0, 0)
    outs = ((GLA_KEY_WIDTH, BF16), (GLA_KEY_WIDTH, BF16), (GLA_VAL_WIDTH, BF16), (GLA_VAL_WIDTH, BF16),
            (GLA_KEY_WIDTH, BF16), (GLA_KEY_WIDTH, BF16))
    return pl.pallas_call(
        _odd_in_kernel,
        grid=(bsz, s // tm),
        in_specs=[
            pl.BlockSpec((1, tm, d), row),
            pl.BlockSpec((1, 3, d), lambda b, i: (b, 0, 0)),
            pl.BlockSpec((1, d), const2),
            pl.BlockSpec(w_in.shape, const2),
            pl.BlockSpec(w_g.shape, const2),
            pl.BlockSpec(b_g.shape, const2),
        ],
        out_specs=[pl.BlockSpec((1, tm, w), row) for w, _ in outs],
        out_shape=[jax.ShapeDtypeStruct((bsz, s, w), dt) for w, dt in outs],
        compiler_params=pltpu.CompilerParams(
            dimension_semantics=("parallel", "parallel"), vmem_limit_bytes=VMEM_LIMIT),
        name="odd_in_proj",
    )(x, mod, nw, w_in, w_g, b_g)


def _cumsum_mats():
    r = np.arange(GLA_CHUNK)[:, None]
    c = np.arange(GLA_CHUNK)[None, :]
    return jnp.asarray(np.stack([c <= r, c >= r]).astype(np.float32), dtype=BF16)


def _chunk_scan(tri, g):
    return _dot(tri, g)


def _decay_cols(row):
    colb = jnp.broadcast_to(row, (GLA_DK, GLA_DK)).T
    return jnp.concatenate([colb] * (GLA_DV // GLA_DK), axis=1)


def _gla_kernel(x_ref, mod_ref, q_ref, k_ref, v_ref, z_ref, gf_ref, gb_ref, tri_ref, gw_ref, wo_ref,
                o_ref, sf_scr, sb_scr, snap_scr, beta_scr, kbd_scr, mix_scr, *, nt):
    tb = x_ref.shape[1]
    nch = tb // GLA_CHUNK
    j = pl.program_id(1)
    c = GLA_CHUNK
    lower, upper = tri_ref[0], tri_ref[1]
    row_i = lax.broadcasted_iota(jnp.int32, (c, c), 0)
    col_i = lax.broadcasted_iota(jnp.int32, (c, c), 1)
    half = c // 2
    causal = col_i <= row_i
    same_half = (row_i < half) == (col_i < half)
    top = lax.broadcasted_iota(jnp.int32, (c, 1), 0) < half

    @pl.when(j == 0)
    def _():
        sf_scr[...] = jnp.zeros_like(sf_scr)
        sb_scr[...] = jnp.zeros_like(sb_scr)
        kbd_scr[...] = jnp.zeros_like(kbd_scr)

    @pl.when(j < nt)
    def _():
        blk = nt - 1 - j
        betas = [_chunk_scan(upper, gb_ref[0, cc * c:(cc + 1) * c, :]) for cc in range(nch)]
        upd, dec = {}, {}
        for cc in range(nch - 1, -1, -1):
            rows = slice(cc * c, (cc + 1) * c)
            beta_scr[pl.ds((blk * nch + cc) * c, c), :] = betas[cc]
            for h in range(GLA_HEADS):
                kc = slice(h * GLA_DK, (h + 1) * GLA_DK)
                vc = slice(h * GLA_DV, (h + 1) * GLA_DV)
                bh = betas[cc][:, kc]
                first = bh[0:1, :]
                kx = k_ref[0, rows, kc] * jnp.exp2(first - bh).astype(BF16)
                upd[cc, h] = _dot_tn(kx, v_ref[0, rows, vc])
                dec[cc, h] = _decay_cols(jnp.exp2(first))
        for h in range(GLA_HEADS):
            st = sb_scr[h]
            for cc in range(nch - 1, -1, -1):
                snap_scr[blk * nch + cc, h] = st.astype(BF16)
                st = dec[cc, h] * st + upd[cc, h]
            sb_scr[h] = st

    @pl.when(j >= nt)
    def _():
        blk = j - nt
        bfw_next = _chunk_scan(lower, gf_ref[0, 0:c, :])
        states = [sf_scr[h] for h in range(GLA_HEADS)]
        for cc in range(nch):
            rows = slice(cc * c, (cc + 1) * c)
            bfw = bfw_next
            beta = beta_scr[pl.ds((blk * nch + cc) * c, c), :]
            a2, off = [], []
            for h in range(GLA_HEADS):
                kc = slice(h * GLA_DK, (h + 1) * GLA_DK)
                qh = q_ref[0, rows, kc]
                kh = k_ref[0, rows, kc]
                bf = bfw[:, kc]
                bb = beta[:, kc]
                rf = jnp.where(top, bf[half // 2:half // 2 + 1, :], bf[half + half // 2:half + half // 2 + 1, :])
                rb = jnp.where(top, bb[half // 2 - 1:half // 2, :], bb[half + half // 2 - 1:half + half // 2, :])
                qcat = jnp.concatenate(
                    [qh * jnp.exp2(bf - rf).astype(BF16), qh * jnp.exp2(bb - rb).astype(BF16)], axis=1)
                kbd_scr[h, 0:c, 0:GLA_DK] = kh * jnp.exp2(rf - bf).astype(BF16)
                kbd_scr[h, c:2 * c, GLA_DK:2 * GLA_DK] = kh * jnp.exp2(rb - bb).astype(BF16)
                a2.append(_dot_nt(qcat, kbd_scr[h]))
                sf_row = bf[half - 1:half, :]
                sb_row = bb[half:half + 1, :]
                q_off = qh * jnp.exp2(jnp.where(top, bb - sb_row, bf - sf_row)).astype(BF16)
                k_off = kh * jnp.exp2(jnp.where(top, sf_row - bf, sb_row - bb)).astype(BF16)
                off.append(_dot_nt(q_off, k_off))
            upd = []
            for h in range(GLA_HEADS):
                kc = slice(h * GLA_DK, (h + 1) * GLA_DK)
                vc = slice(h * GLA_DV, (h + 1) * GLA_DV)
                bf = bfw[:, kc]
                kx = k_ref[0, rows, kc] * jnp.exp2(bf[c - 1:c, :] - bf).astype(BF16)
                upd.append(_dot_tn(kx, v_ref[0, rows, vc]))
            if cc + 1 < nch:
                bfw_next = _chunk_scan(lower, gf_ref[0, (cc + 1) * c:(cc + 2) * c, :])
            outs = []
            for h in range(GLA_HEADS):
                kc = slice(h * GLA_DK, (h + 1) * GLA_DK)
                vc = slice(h * GLA_DV, (h + 1) * GLA_DV)
                bf = bfw[:, kc]
                bb = beta[:, kc]
                att = jnp.where(same_half, jnp.where(causal, a2[h][:, 0:c], a2[h][:, c:2 * c]), off[h])
                att = att.astype(BF16)
                qh = q_ref[0, rows, kc]
                lhs = jnp.concatenate(
                    [att, qh * jnp.exp2(bf).astype(BF16), qh * jnp.exp2(bb).astype(BF16)], axis=1)
                rhs = jnp.concatenate(
                    [v_ref[0, rows, vc], states[h].astype(BF16), snap_scr[blk * nch + cc, h]], axis=0)
                outs.append(_dot(lhs, rhs))
                states[h] = _decay_cols(jnp.exp2(bf[c - 1:c, :])) * states[h] + upd[h]
            for h in range(GLA_HEADS):
                vc = slice(h * GLA_DV, (h + 1) * GLA_DV)
                o = outs[h]
                ms = jnp.mean(o * o, axis=-1, keepdims=True)
                on = (o * lax.rsqrt(ms + EPS)) * gw_ref[...]
                mix_scr[rows, vc] = on.astype(BF16) * z_ref[0, rows, vc]
        for h in range(GLA_HEADS):
            sf_scr[h] = states[h]
        y = _dot(mix_scr[...], wo_ref[...])
        gate = mod_ref[0][2:3, :]
        o_ref[0] = x_ref[0] + gate * y


def _gla_mix(x, mod, q, k, v, z, gf, gb, gw, w_out, tb):
    bsz, s, d = x.shape
    nt = s // tb
    const2 = lambda b, j: (0, 0)
    both = lambda b, j: (b, jnp.where(j < nt, nt - 1 - j, j - nt), 0)
    fwd_only = lambda b, j: (b, jnp.maximum(j - nt, 0), 0)
    return pl.pallas_call(
        functools.partial(_gla_kernel, nt=nt),
        grid=(bsz, 2 * nt),
        in_specs=[
            pl.BlockSpec((1, tb, d), fwd_only),
            pl.BlockSpec((1, 3, d), lambda b, j: (b, 0, 0)),
            pl.BlockSpec((1, tb, GLA_KEY_WIDTH), fwd_only),
            pl.BlockSpec((1, tb, GLA_KEY_WIDTH), both),
            pl.BlockSpec((1, tb, GLA_VAL_WIDTH), both),
            pl.BlockSpec((1, tb, GLA_VAL_WIDTH), fwd_only),
            pl.BlockSpec((1, tb, GLA_KEY_WIDTH), fwd_only),
            pl.BlockSpec((1, tb, GLA_KEY_WIDTH), lambda b, j: (b, jnp.maximum(nt - 1 - j, 0), 0)),
            pl.BlockSpec((2, GLA_CHUNK, GLA_CHUNK), lambda b, j: (0, 0, 0)),
            pl.BlockSpec((1, GLA_DV), const2),
            pl.BlockSpec(w_out.shape, const2),
        ],
        out_specs=pl.BlockSpec((1, tb, d), fwd_only),
        out_shape=jax.ShapeDtypeStruct((bsz, s, d), F32),
        scratch_shapes=[
            pltpu.VMEM((GLA_HEADS, GLA_DK, GLA_DV), F32),
            pltpu.VMEM((GLA_HEADS, GLA_DK, GLA_DV), F32),
            pltpu.VMEM((s // GLA_CHUNK, GLA_HEADS, GLA_DK, GLA_DV), BF16),
            pltpu.VMEM((s, GLA_KEY_WIDTH), F32),
            pltpu.VMEM((GLA_HEADS, 2 * GLA_CHUNK, 2 * GLA_DK), BF16),
            pltpu.VMEM((tb, d), BF16),
        ],
        compiler_params=pltpu.CompilerParams(
            dimension_semantics=("parallel", "arbitrary"), vmem_limit_bytes=VMEM_LIMIT),
        name="gla_mix",
    )(x, mod, q, k, v, z, gf, gb, _cumsum_mats(), gw, w_out)


def kernel(x, c, norm_w, w_ada, b_ada, w_in_a, w_pool, pool_scale, q_norm_w, k_norm_w, attn_sink,
           w_out_a, w_in_c, w_gate_up, b_gate, gla_norm_w, w_out_c):
    bsz, s, d = x.shape
    assert d == D_MODEL and s % 512 == 0
    mod_all = _modulation(c, w_ada, b_ada).reshape(DEPTH, bsz, 3, d)
    bias = _attn_bias()
    band = _pool_band()
    icnt = _pool_inv_count(s)
    zeros_g = jnp.zeros((GLA_GATE_RANK, GLA_KEY_WIDTH), F32)
    for l in range(DEPTH):
        mod = mod_all[l]
        nw = norm_w[l].reshape(1, d)
        if l % 2 == 0:
            i = l // 2
            qwt = jnp.broadcast_to(
                jnp.tile(q_norm_w[i] * (LOG2E * ATT_HEAD_DIM ** -0.5), ATT_HEADS)[:, None],
                (ATT_WIDTH, SUB_ROWS))
            kw2 = jnp.tile(k_norm_w[i], 2).reshape(1, LANES)
            sink = jnp.repeat(LOG2E * attn_sink[i].reshape(ATT_KV_HEADS, ATT_GROUP), BLOCK, axis=1)
            u, zp, qt, k, vt, za = _even_in_proj(
                x, mod, nw, _pool_fold(w_in_a, w_pool, i), qwt, kw2, tm=2048)
            x = _even_mix(x, mod, u, zp, qt, k, vt, za, band, icnt,
                          pool_scale[i].reshape(1, POOL_WIDTH), bias, sink,
                          w_out_a[i].astype(BF16), tq=1024)
        else:
            jdx = l // 2
            w_g = jnp.concatenate([
                jnp.concatenate([w_gate_up[jdx, 0], zeros_g], axis=1),
                jnp.concatenate([zeros_g, w_gate_up[jdx, 1]], axis=1)], axis=0)
            w_g = (LOG2E * w_g).astype(BF16)
            b_g = LOG2E * b_gate[jdx].reshape(1, 2 * GLA_KEY_WIDTH)
            q, k, v, z, gf, gb = _odd_in_proj(x, mod, nw, w_in_c[jdx].astype(BF16), w_g, b_g, tm=1024)
            x = _gla_mix(x, mod, q, k, v, z, gf, gb, gla_norm_w[jdx].reshape(1, GLA_DV),
                         w_out_c[jdx].astype(BF16), tb=1024)
    return x
```

```python
import functools

import jax
import jax.numpy as jnp
import numpy as np
from jax import lax
from jax.experimental import pallas as pl
from jax.experimental.pallas import tpu as pltpu

F32 = jnp.float32
BF16 = jnp.bfloat16

D_MODEL = 1024
DEPTH = 4
POOL_WINDOWS = (2, 4, 8, 16)
POOL_WIDTH = 512
POOL_GROUP_DIM = 128
ATT_HEADS = 8
ATT_KV_HEADS = 2
ATT_GROUP = ATT_HEADS // ATT_KV_HEADS
ATT_HEAD_DIM = 64
ATT_WIDTH = 512
ATT_KV_WIDTH = 128
WINDOW = 128
BLOCK = 128
GLA_HEADS = 4
GLA_KEY_WIDTH = 512
GLA_VAL_WIDTH = 1024
GLA_DK = 128
GLA_DV = 256
GLA_GATE_RANK = 16
GLA_GATE_NORMALIZER = 16.0
EPS = 1e-6
NEG = -1e30
LOG2E = 1.4426950408889634

LANES = 128
POOL_HALO = 64
GLA_CHUNK = 128
SUB_ROWS = 512
ONES_ROWS = 16
VMEM_LIMIT = 56 * 1024 * 1024


def _dot(a, b):
    return jnp.dot(a, b, preferred_element_type=F32)


def _dot_nt(a, b):
    return lax.dot_general(a, b, (((1,), (1,)), ((), ())), preferred_element_type=F32)


def _dot_tn(a, b):
    return lax.dot_general(a, b, (((0,), (0,)), ((), ())), preferred_element_type=F32)


def _split_bf16(x):
    hi = x.astype(BF16)
    lo = (x - hi.astype(F32)).astype(BF16)
    return hi, lo


def _silu(x):
    h = 0.5 * x
    return h + h * jnp.tanh(h)


def _mod_kernel(c_ref, w_ref, b_ref, o_ref):
    sc = _silu(c_ref[...])
    sc_hi, sc_lo = _split_bf16(sc)
    w_hi, w_lo = _split_bf16(w_ref[0])
    acc = _dot(sc_hi, w_hi) + _dot(sc_lo, w_hi) + _dot(sc_hi, w_lo)
    o_ref[0] = acc + b_ref[0]


def _modulation(c, w_ada, b_ada):
    depth, d, n = w_ada.shape
    bsz = c.shape[0]
    tn = 1024
    return pl.pallas_call(
        _mod_kernel,
        grid=(depth, n // tn),
        in_specs=[
            pl.BlockSpec((bsz, d), lambda l, j: (0, 0)),
            pl.BlockSpec((1, d, tn), lambda l, j: (l, 0, j)),
            pl.BlockSpec((1, 1, tn), lambda l, j: (l, 0, j)),
        ],
        out_specs=pl.BlockSpec((1, bsz, tn), lambda l, j: (l, 0, j)),
        out_shape=jax.ShapeDtypeStruct((depth, bsz, n), F32),
        compiler_params=pltpu.CompilerParams(
            dimension_semantics=("parallel", "parallel"), vmem_limit_bytes=VMEM_LIMIT),
        name="adaln_mod",
    )(c, w_ada, b_ada.reshape(depth, 1, n))


def _pool_fold_kernel(w_ref, wp_ref, o_ref, *, groups):
    g = pl.program_id(0)

    @pl.when(g < groups)
    def _():
        w_hi, w_lo = _split_bf16(w_ref[0])
        p_hi, p_lo = _split_bf16(wp_ref[0, 0])
        o_ref[...] = (_dot(w_hi, p_hi) + _dot(w_lo, p_hi) + _dot(w_hi, p_lo)).astype(BF16)

    @pl.when(g >= groups)
    def _():
        o_ref[...] = w_ref[0].astype(BF16)


def _pool_fold(w_in_all, w_pool_all, layer):
    _, d, n = w_in_all.shape
    _, groups, gd, _ = w_pool_all.shape
    return pl.pallas_call(
        functools.partial(_pool_fold_kernel, groups=groups),
        grid=(n // gd,),
        in_specs=[
            pl.BlockSpec((1, d, gd), lambda g: (layer, 0, g)),
            pl.BlockSpec((1, 1, gd, gd), lambda g: (layer, jnp.minimum(g, groups - 1), 0, 0)),
        ],
        out_specs=pl.BlockSpec((d, gd), lambda g: (0, g)),
        out_shape=jax.ShapeDtypeStruct((d, n), BF16),
        compiler_params=pltpu.CompilerParams(dimension_semantics=("parallel",)),
        name="pool_fold",
    )(w_in_all, w_pool_all)


def _two_stage_pipeline(n, produce, consume):
    always = pl.program_id(0) >= 0
    produce(0)
    for r in range(n):
        @pl.when(always)
        def _(r=r):
            if r + 1 < n:
                produce(r + 1)
            consume(r)


def _norm_modulate(x, nw, mod):
    ms = jnp.mean(x * x, axis=-1, keepdims=True)
    y = (x * lax.rsqrt(ms + EPS)) * nw
    shift = mod[0:1, :]
    scale = mod[1:2, :]
    return (y * (1.0 + scale) + shift).astype(BF16)


def _head_rms(p, w2, head_dim):
    assert 2 * head_dim == LANES and p.shape[-1] == LANES
    lane = lax.broadcasted_iota(jnp.int32, (1, LANES), 1)
    first = lane < head_dim
    sq = p * p
    s0 = jnp.sum(jnp.where(first, sq, 0.0), axis=-1, keepdims=True)
    s1 = jnp.sum(jnp.where(first, 0.0, sq), axis=-1, keepdims=True)
    ms = jnp.where(first, s0, s1) * (1.0 / head_dim)
    return (p * lax.rsqrt(ms + EPS)) * w2


def _even_in_kernel(x_ref, mod_ref, nw_ref, w_ref, qwt_ref, kw_ref,
                    u_ref, zp_ref, qt_ref, k_ref, vt_ref, za_ref):
    c_u, c_zp, c_q = 0, POOL_WIDTH, 2 * POOL_WIDTH
    c_kv = c_q + ATT_WIDTH
    c_za = c_kv + 2 * ATT_KV_WIDTH
    for r in range(x_ref.shape[1] // SUB_ROWS):
        tok = slice(r * SUB_ROWS, (r + 1) * SUB_ROWS)
        hb = _norm_modulate(x_ref[0, tok, :], nw_ref[...], mod_ref[0])
        pqt = _dot(hb, w_ref[:, c_q:c_q + ATT_WIDTH]).T
        for h in range(ATT_HEADS):
            rows = slice(h * ATT_HEAD_DIM, (h + 1) * ATT_HEAD_DIM)
            ph = pqt[rows, :]
            ms = jnp.mean(ph * ph, axis=0, keepdims=True)
            qt_ref[0, rows, tok] = ((ph * lax.rsqrt(ms + EPS)) * qwt_ref[rows, :]).astype(BF16)
        pkv = _dot(hb, w_ref[:, c_kv:c_kv + 2 * ATT_KV_WIDTH])
        k_ref[0, tok, :] = _head_rms(pkv[:, 0:ATT_KV_WIDTH], kw_ref[...], ATT_HEAD_DIM).astype(BF16)
        vt_ref[0, :, tok] = pkv[:, ATT_KV_WIDTH:].T.astype(BF16)
        zp_ref[0, tok, :] = _silu(_dot(hb, w_ref[:, c_zp:c_zp + POOL_WIDTH])).astype(BF16)
        za_ref[0, tok, :] = _silu(_dot(hb, w_ref[:, c_za:c_za + ATT_WIDTH])).astype(BF16)
        u_ref[0, tok, :] = _dot(hb, w_ref[:, c_u:c_u + POOL_WIDTH]).astype(BF16)


def _even_in_proj(x, mod, nw, w_in, qwt, kw2, tm):
    bsz, s, d = x.shape
    n = w_in.shape[1]
    row = lambda b, i: (b, i, 0)
    col = lambda b, i: (b, 0, i)
    const2 = lambda b, i: (0, 0)
    outs = (((s, POOL_WIDTH), (tm, POOL_WIDTH), row), ((s, POOL_WIDTH), (tm, POOL_WIDTH), row),
            ((ATT_WIDTH, s), (ATT_WIDTH, tm), col), ((s, ATT_KV_WIDTH), (tm, ATT_KV_WIDTH), row),
            ((ATT_KV_WIDTH, s), (ATT_KV_WIDTH, tm), col), ((s, ATT_WIDTH), (tm, ATT_WIDTH), row))
    return pl.pallas_call(
        _even_in_kernel,
        grid=(bsz, s // tm),
        in_specs=[
            pl.BlockSpec((1, tm, d), row),
            pl.BlockSpec((1, 3, d), lambda b, i: (b, 0, 0)),
            pl.BlockSpec((1, d), const2),
            pl.BlockSpec((d, n), const2),
            pl.BlockSpec((ATT_WIDTH, SUB_ROWS), const2),
            pl.BlockSpec((1, LANES), const2),
        ],
        out_specs=[pl.BlockSpec((1,) + blk, imap) for _, blk, imap in outs],
        out_shape=[jax.ShapeDtypeStruct((bsz,) + full, BF16) for full, _, _ in outs],
        compiler_params=pltpu.CompilerParams(
            dimension_semantics=("parallel", "parallel"), vmem_limit_bytes=VMEM_LIMIT),
        name="even_in_proj",
    )(x, mod, nw, w_in, qwt, kw2)


def _even_mix_kernel(x_ref, mod_ref, u_ref, up_ref, un_ref, zp_ref, qt_ref, k_ref, kp_ref, kn_ref,
                     vt_ref, vtp_ref, vtn_ref, za_ref, band_ref, icnt_ref, ps_ref, bias_ref, sink_ref,
                     wo_ref, o_ref, ue_scr, ke_scr, vte_scr, st_scr, m_scr, ot_scr, mix_scr,
                     xbuf, xsem, *, nt, total):
    tq = o_ref.shape[1]
    i = pl.program_id(1)
    step = pl.program_id(0) * nt + i

    def x_copy(s):
        return pltpu.make_async_copy(
            x_ref.at[s // nt, pl.ds((s % nt) * tq, tq), :], xbuf.at[s % 3], xsem.at[s % 3])

    @pl.when(step == 0)
    def _():
        x_copy(step).start()
        if total > 1:
            x_copy(step + 1).start()

    @pl.when(step + 2 < total)
    def _():
        x_copy(step + 2).start()
    last = pl.num_programs(1) - 1
    nblk = tq // BLOCK
    kspan = 3 * BLOCK

    ue_scr[0:POOL_HALO, :] = jnp.where(i > 0, up_ref[0], jnp.zeros_like(up_ref[0]))
    ue_scr[POOL_HALO:POOL_HALO + tq, :] = u_ref[0]
    ue_scr[POOL_HALO + tq:, :] = jnp.where(i < last, un_ref[0], jnp.zeros_like(un_ref[0]))
    tots = {}
    for j in range(0, nblk, 2):
        for g in range(len(POOL_WINDOWS)):
            cols = slice(g * POOL_GROUP_DIM, (g + 1) * POOL_GROUP_DIM)
            wins = [ue_scr[jj * BLOCK:jj * BLOCK + BLOCK + 2 * POOL_HALO, cols] for jj in (j, j + 1)]
            both = _dot(band_ref[g], jnp.concatenate(wins, axis=1))
            tots[j, g] = both[:, 0:POOL_GROUP_DIM]
            tots[j + 1, g] = both[:, POOL_GROUP_DIM:]
    for j in range(nblk):
        rows = slice(j * BLOCK, (j + 1) * BLOCK)
        for g, w in enumerate(POOL_WINDOWS):
            cols = slice(g * POOL_GROUP_DIM, (g + 1) * POOL_GROUP_DIM)
            inv_cnt = 1.0 / w
            if j == 0:
                inv_cnt = jnp.where(i == 0, icnt_ref[0, :, cols], inv_cnt)
            if j == nblk - 1:
                inv_cnt = jnp.where(i == last, icnt_ref[1, :, cols], inv_cnt)
            y = (tots[j, g] * inv_cnt - u_ref[0, rows, cols].astype(F32)) * ps_ref[:, cols]
            mix_scr[rows, cols] = (y * zp_ref[0, rows, cols].astype(F32)).astype(BF16)

    ke_scr[0:BLOCK, :] = kp_ref[0]
    ke_scr[BLOCK:BLOCK + tq, :] = k_ref[0]
    ke_scr[BLOCK + tq:, :] = kn_ref[0]
    for kh in range(ATT_KV_HEADS):
        hd = slice(kh * ATT_HEAD_DIM, (kh + 1) * ATT_HEAD_DIM)
        vte_scr[kh, 0:ATT_HEAD_DIM, 0:BLOCK] = vtp_ref[0, hd, :]
        vte_scr[kh, 0:ATT_HEAD_DIM, BLOCK:BLOCK + tq] = vt_ref[0, hd, :]
        vte_scr[kh, 0:ATT_HEAD_DIM, BLOCK + tq:] = vtn_ref[0, hd, :]
        vte_scr[kh, ATT_HEAD_DIM:, :] = jnp.ones((ONES_ROWS, tq + 2 * BLOCK), BF16)

    def score_stage(j):
        rows = slice(j * BLOCK, (j + 1) * BLOCK)
        top = jnp.where(i == 0, 3, 0) if j == 0 else 0
        bot = jnp.where(i == last, 3, 2) if j == nblk - 1 else 2
        for kh in range(ATT_KV_HEADS):
            hd = slice(kh * ATT_HEAD_DIM, (kh + 1) * ATT_HEAD_DIM)
            qs = jnp.concatenate(
                [qt_ref[0, (kh * ATT_GROUP + g) * ATT_HEAD_DIM:(kh * ATT_GROUP + g + 1) * ATT_HEAD_DIM, rows]
                 for g in range(ATT_GROUP)], axis=1)
            qk = _dot(ke_scr[j * BLOCK:j * BLOCK + kspan, hd], qs)
            st = jnp.concatenate(
                [qk[kb * BLOCK:(kb + 1) * BLOCK] + bias_ref[kh * 4 + blk]
                 for kb, blk in enumerate((top, 1, bot))], axis=0)
            st_scr[j % 2, kh] = st
            m_scr[j % 2, kh] = jnp.maximum(jnp.max(st, axis=0, keepdims=True), sink_ref[kh:kh + 1, :])

    def value_stage(j):
        rows = slice(j * BLOCK, (j + 1) * BLOCK)
        for kh in range(ATT_KV_HEADS):
            m = m_scr[j % 2, kh]
            p = jnp.exp2(st_scr[j % 2, kh] - m).astype(BF16)
            pv = _dot(vte_scr[kh, :, j * BLOCK:j * BLOCK + kspan], p)
            den = pv[ATT_HEAD_DIM:ATT_HEAD_DIM + 1, :] + jnp.exp2(sink_ref[kh:kh + 1, :] - m)
            ot = pv[0:ATT_HEAD_DIM, :] * (1.0 / den)
            for g in range(ATT_GROUP):
                h = kh * ATT_GROUP + g
                ot_scr[h * ATT_HEAD_DIM:(h + 1) * ATT_HEAD_DIM, :] = ot[:, g * BLOCK:(g + 1) * BLOCK]
        o = ot_scr[...].T
        mix_scr[rows, POOL_WIDTH:] = (o * za_ref[0, rows, :].astype(F32)).astype(BF16)

    _two_stage_pipeline(nblk, score_stage, value_stage)

    @pl.when(pl.program_id(0) >= 0)
    def _():
        y = _dot(mix_scr[...], wo_ref[...])
        gate = mod_ref[0][2:3, :]
        x_copy(step).wait()
        o_ref[0] = xbuf[step % 3] + gate * y


def _even_mix(x, mod, u, zp, qt, k, vt, za, band, icnt, pool_scale, bias, sink, w_out, tq):
    bsz, s, d = x.shape
    nt = s // tq
    row = lambda b, i: (b, i, 0)
    col = lambda b, i: (b, 0, i)
    const2 = lambda b, i: (0, 0)
    const3 = lambda b, i: (0, 0, 0)
    hp = tq // POOL_HALO
    hb = tq // BLOCK
    prev_row = lambda n: (lambda b, i: (b, jnp.maximum(i * n - 1, 0), 0))
    next_row = lambda n, tot: (lambda b, i: (b, jnp.minimum((i + 1) * n, tot - 1), 0))
    return pl.pallas_call(
        functools.partial(_even_mix_kernel, nt=nt, total=bsz * nt),
        grid=(bsz, nt),
        in_specs=[
            pl.BlockSpec(memory_space=pl.ANY),
            pl.BlockSpec((1, 3, d), lambda b, i: (b, 0, 0)),
            pl.BlockSpec((1, tq, POOL_WIDTH), row),
            pl.BlockSpec((1, POOL_HALO, POOL_WIDTH), prev_row(hp)),
            pl.BlockSpec((1, POOL_HALO, POOL_WIDTH), next_row(hp, s // POOL_HALO)),
            pl.BlockSpec((1, tq, POOL_WIDTH), row),
            pl.BlockSpec((1, ATT_WIDTH, tq), col),
            pl.BlockSpec((1, tq, ATT_KV_WIDTH), row),
            pl.BlockSpec((1, BLOCK, ATT_KV_WIDTH), prev_row(hb)),
            pl.BlockSpec((1, BLOCK, ATT_KV_WIDTH), next_row(hb, s // BLOCK)),
            pl.BlockSpec((1, ATT_KV_WIDTH, tq), col),
            pl.BlockSpec((1, ATT_KV_WIDTH, BLOCK), lambda b, i: (b, 0, jnp.maximum(i * hb - 1, 0))),
            pl.BlockSpec((1, ATT_KV_WIDTH, BLOCK), lambda b, i: (b, 0, jnp.minimum((i + 1) * hb, s // BLOCK - 1))),
            pl.BlockSpec((1, tq, ATT_WIDTH), row),
            pl.BlockSpec(band.shape, const3),
            pl.BlockSpec(icnt.shape, const3),
            pl.BlockSpec((1, POOL_WIDTH), const2),
            pl.BlockSpec(bias.shape, const3),
            pl.BlockSpec(sink.shape, const2),
            pl.BlockSpec(w_out.shape, const2),
        ],
        out_specs=pl.BlockSpec((1, tq, d), row),
        out_shape=jax.ShapeDtypeStruct((bsz, s, d), F32),
        scratch_shapes=[
            pltpu.VMEM((tq + 2 * POOL_HALO, POOL_WIDTH), BF16),
            pltpu.VMEM((tq + 2 * BLOCK, ATT_KV_WIDTH), BF16),
            pltpu.VMEM((ATT_KV_HEADS, ATT_HEAD_DIM + ONES_ROWS, tq + 2 * BLOCK), BF16),
            pltpu.VMEM((2, ATT_KV_HEADS, 3 * BLOCK, ATT_GROUP * BLOCK), F32),
            pltpu.VMEM((2, ATT_KV_HEADS, 1, ATT_GROUP * BLOCK), F32),
            pltpu.VMEM((ATT_WIDTH, BLOCK), F32),
            pltpu.VMEM((tq, d), BF16),
            pltpu.VMEM((3, tq, d), F32),
            pltpu.SemaphoreType.DMA((3,)),
        ],
        compiler_params=pltpu.CompilerParams(
            dimension_semantics=("arbitrary", "arbitrary"), vmem_limit_bytes=VMEM_LIMIT),
        name="even_mix",
    )(x, mod, u, u, u, zp, qt, k, k, k, vt, vt, vt, za, band, icnt, pool_scale, bias, sink, w_out)


def _attn_bias():
    r = np.arange(BLOCK)[:, None]
    c = np.arange(3 * BLOCK)[None, :]
    dist = np.abs(r + BLOCK - c)
    slopes = 2.0 ** (-8.0 * np.arange(1, ATT_HEADS + 1, dtype=np.float64) / ATT_HEADS)
    bias = -(LOG2E * slopes)[:, None, None] * dist.astype(np.float64)[None]
    bias = np.where((dist <= WINDOW)[None], bias, NEG)
    bias = bias.reshape(ATT_KV_HEADS, ATT_GROUP, BLOCK, 3, BLOCK)
    bias = np.transpose(bias, (0, 3, 4, 1, 2)).reshape(ATT_KV_HEADS, 3, BLOCK, ATT_GROUP * BLOCK)
    masked = np.full((ATT_KV_HEADS, 1, BLOCK, ATT_GROUP * BLOCK), NEG)
    bias = np.concatenate([bias, masked], axis=1).reshape(ATT_KV_HEADS * 4, BLOCK, ATT_GROUP * BLOCK)
    return jnp.asarray(bias, dtype=F32)


def _pool_band():
    r = np.arange(BLOCK)[:, None]
    c = np.arange(BLOCK + 2 * POOL_HALO)[None, :]
    off = c - POOL_HALO - r
    band = np.stack([((off >= -(w // 2)) & (off <= w // 2 - 1)) for w in POOL_WINDOWS])
    return jnp.asarray(band.astype(np.float32), dtype=BF16)


def _pool_inv_count(seq_len):
    t = np.concatenate([np.arange(BLOCK), np.arange(seq_len - BLOCK, seq_len)])[:, None]
    w = np.repeat(np.asarray(POOL_WINDOWS), POOL_GROUP_DIM)[None, :]
    lo = np.clip(t - w // 2, 0, seq_len - 1)
    hi = np.clip(t + w // 2 - 1, 0, seq_len - 1)
    inv = np.float32(1.0) / (hi - lo + 1).astype(np.float32)
    return jnp.asarray(inv.reshape(2, BLOCK, POOL_WIDTH), dtype=F32)


def _log2_sigmoid(y):
    sign_bit = jnp.uint32(1 << 31)
    neg_abs = lax.bitcast_convert_type(lax.bitcast_convert_type(y, jnp.uint32) | sign_bit, F32)
    return jnp.minimum(y, 0.0) - jnp.log2(1.0 + jnp.exp2(neg_abs))


def _odd_in_kernel(x_ref, mod_ref, nw_ref, w_ref, wg_ref, bg_ref,
                   q_ref, k_ref, v_ref, z_ref, gf_ref, gb_ref):
    c_q, c_k, c_v = 0, GLA_KEY_WIDTH, 2 * GLA_KEY_WIDTH
    c_z = c_v + GLA_VAL_WIDTH
    c_a = c_z + GLA_VAL_WIDTH
    for r in range(x_ref.shape[1] // SUB_ROWS):
        tok = slice(r * SUB_ROWS, (r + 1) * SUB_ROWS)
        hb = _norm_modulate(x_ref[0, tok, :], nw_ref[...], mod_ref[0])
        q_ref[0, tok, :] = (_dot(hb, w_ref[:, c_q:c_q + GLA_KEY_WIDTH]) * (GLA_DK ** -0.5)).astype(BF16)
        k_ref[0, tok, :] = _dot(hb, w_ref[:, c_k:c_k + GLA_KEY_WIDTH]).astype(BF16)
        v_ref[0, tok, :] = _dot(hb, w_ref[:, c_v:c_v + GLA_VAL_WIDTH]).astype(BF16)
        z_ref[0, tok, :] = _silu(_dot(hb, w_ref[:, c_z:c_z + GLA_VAL_WIDTH])).astype(BF16)
        a = _dot(hb, w_ref[:, c_a:c_a + 2 * GLA_GATE_RANK]).astype(BF16)
        logits = _dot(a, wg_ref[...]) + bg_ref[...]
        g = _log2_sigmoid(logits) * (1.0 / GLA_GATE_NORMALIZER)
        gf_ref[0, tok, :] = g[:, 0:GLA_KEY_WIDTH].astype(BF16)
        gb_ref[0, tok, :] = g[:, GLA_KEY_WIDTH:].astype(BF16)


def _odd_in_proj(x, mod, nw, w_in, w_g, b_g, tm):
    bsz, s, d = x.shape
    row = lambda b, i: (b, i, 0)
    const2 = lambda b, i: (0, 0)
    outs = ((GLA_KEY_WIDTH, BF16), (GLA_KEY_WIDTH, BF16), (GLA_VAL_WIDTH, BF16), (GLA_VAL_WIDTH, BF16),
            (GLA_KEY_WIDTH, BF16), (GLA_KEY_WIDTH, BF16))
    return pl.pallas_call(
        _odd_in_kernel,
        grid=(bsz, s // tm),
        in_specs=[
            pl.BlockSpec((1, tm, d), row),
            pl.BlockSpec((1, 3, d), lambda b, i: (b, 0, 0)),
            pl.BlockSpec((1, d), const2),
            pl.BlockSpec(w_in.shape, const2),
            pl.BlockSpec(w_g.shape, const2),
            pl.BlockSpec(b_g.shape, const2),
        ],
        out_specs=[pl.BlockSpec((1, tm, w), row) for w, _ in outs],
        out_shape=[jax.ShapeDtypeStruct((bsz, s, w), dt) for w, dt in outs],
        compiler_params=pltpu.CompilerParams(
            dimension_semantics=("parallel", "parallel"), vmem_limit_bytes=VMEM_LIMIT),
        name="odd_in_proj",
    )(x, mod, nw, w_in, w_g, b_g)


def _cumsum_mats():
    r = np.arange(GLA_CHUNK)[:, None]
    c = np.arange(GLA_CHUNK)[None, :]
    return jnp.asarray(np.stack([c <= r, c >= r]).astype(np.float32), dtype=BF16)


def _chunk_scan(tri, g):
    return _dot(tri, g)


def _decay_cols(row):
    colb = jnp.broadcast_to(row, (GLA_DK, GLA_DK)).T
    return jnp.concatenate([colb] * (GLA_DV // GLA_DK), axis=1)


def _gla_kernel(x_ref, mod_ref, q_ref, k_ref, v_ref, z_ref, gf_ref, gb_ref, tri_ref, gw_ref, wo_ref,
                o_ref, sf_scr, sb_scr, snap_scr, beta_scr, kbd_scr, mix_scr, *, nt):
    tb = x_ref.shape[1]
    nch = tb // GLA_CHUNK
    j = pl.program_id(1)
    c = GLA_CHUNK
    lower, upper = tri_ref[0], tri_ref[1]
    row_i = lax.broadcasted_iota(jnp.int32, (c, c), 0)
    col_i = lax.broadcasted_iota(jnp.int32, (c, c), 1)
    half = c // 2
    causal = col_i <= row_i
    same_half = (row_i < half) == (col_i < half)
    top = lax.broadcasted_iota(jnp.int32, (c, 1), 0) < half

    @pl.when(j == 0)
    def _():
        sf_scr[...] = jnp.zeros_like(sf_scr)
        sb_scr[...] = jnp.zeros_like(sb_scr)
        kbd_scr[...] = jnp.zeros_like(kbd_scr)

    @pl.when(j < nt)
    def _():
        blk = nt - 1 - j
        betas = [_chunk_scan(upper, gb_ref[0, cc * c:(cc + 1) * c, :]) for cc in range(nch)]
        upd, dec = {}, {}
        for cc in range(nch - 1, -1, -1):
            rows = slice(cc * c, (cc + 1) * c)
            beta_scr[pl.ds((blk * nch + cc) * c, c), :] = betas[cc]
            for h in range(GLA_HEADS):
                kc = slice(h * GLA_DK, (h + 1) * GLA_DK)
                vc = slice(h * GLA_DV, (h + 1) * GLA_DV)
                bh = betas[cc][:, kc]
                first = bh[0:1, :]
                kx = k_ref[0, rows, kc] * jnp.exp2(first - bh).astype(BF16)
                upd[cc, h] = _dot_tn(kx, v_ref[0, rows, vc])
                dec[cc, h] = _decay_cols(jnp.exp2(first))
        for h in range(GLA_HEADS):
            st = sb_scr[h]
            for cc in range(nch - 1, -1, -1):
                snap_scr[blk * nch + cc, h] = st.astype(BF16)
                st = dec[cc, h] * st + upd[cc, h]
            sb_scr[h] = st

    @pl.when(j >= nt)
    def _():
        blk = j - nt
        bfw_next = _chunk_scan(lower, gf_ref[0, 0:c, :])
        states = [sf_scr[h] for h in range(GLA_HEADS)]
        for cc in range(nch):
            rows = slice(cc * c, (cc + 1) * c)
            bfw = bfw_next
            beta = beta_scr[pl.ds((blk * nch + cc) * c, c), :]
            a2, off = [], []
            for h in range(GLA_HEADS):
                kc = slice(h * GLA_DK, (h + 1) * GLA_DK)
                qh = q_ref[0, rows, kc]
                kh = k_ref[0, rows, kc]
                bf = bfw[:, kc]
                bb = beta[:, kc]
                rf = jnp.where(top, bf[half // 2:half // 2 + 1, :], bf[half + half // 2:half + half // 2 + 1, :])
                rb = jnp.where(top, bb[half // 2 - 1:half // 2, :], bb[half + half // 2 - 1:half + half // 2, :])
                qcat = jnp.concatenate(
                    [qh * jnp.exp2(bf - rf).astype(BF16), qh * jnp.exp2(bb - rb).astype(BF16)], axis=1)
                kbd_scr[h, 0:c, 0:GLA_DK] = kh * jnp.exp2(rf - bf).astype(BF16)
                kbd_scr[h, c:2 * c, GLA_DK:2 * GLA_DK] = kh * jnp.exp2(rb - bb).astype(BF16)
                a2.append(_dot_nt(qcat, kbd_scr[h]))
                sf_row = bf[half - 1:half, :]
                sb_row = bb[half:half + 1, :]
                q_off = qh * jnp.exp2(jnp.where(top, bb - sb_row, bf - sf_row)).astype(BF16)
                k_off = kh * jnp.exp2(jnp.where(top, sf_row - bf, sb_row - bb)).astype(BF16)
                off.append(_dot_nt(q_off, k_off))
            upd = []
            for h in range(GLA_HEADS):
                kc = slice(h * GLA_DK, (h + 1) * GLA_DK)
                vc = slice(h * GLA_DV, (h + 1) * GLA_DV)
                bf = bfw[:, kc]
                kx = k_ref[0, rows, kc] * jnp.exp2(bf[c - 1:c, :] - bf).astype(BF16)
                upd.append(_dot_tn(kx, v_ref[0, rows, vc]))
            if cc + 1 < nch:
                bfw_next = _chunk_scan(lower, gf_ref[0, (cc + 1) * c:(cc + 2) * c, :])
            outs = []
            for h in range(GLA_HEADS):
                kc = slice(h * GLA_DK, (h + 1) * GLA_DK)
                vc = slice(h * GLA_DV, (h + 1) * GLA_DV)
                bf = bfw[:, kc]
                bb = beta[:, kc]
                att = jnp.where(same_half, jnp.where(causal, a2[h][:, 0:c], a2[h][:, c:2 * c]), off[h])
                att = att.astype(BF16)
                qh = q_ref[0, rows, kc]
                lhs = jnp.concatenate(
                    [att, qh * jnp.exp2(bf).astype(BF16), qh * jnp.exp2(bb).astype(BF16)], axis=1)
                rhs = jnp.concatenate(
                    [v_ref[0, rows, vc], states[h].astype(BF16), snap_scr[blk * nch + cc, h]], axis=0)
                outs.append(_dot(lhs, rhs))
                states[h] = _decay_cols(jnp.exp2(bf[c - 1:c, :])) * states[h] + upd[h]
            for h in range(GLA_HEADS):
                vc = slice(h * GLA_DV, (h + 1) * GLA_DV)
                o = outs[h]
                ms = jnp.mean(o * o, axis=-1, keepdims=True)
                on = (o * lax.rsqrt(ms + EPS)) * gw_ref[...]
                mix_scr[rows, vc] = on.astype(BF16) * z_ref[0, rows, vc]
        for h in range(GLA_HEADS):
            sf_scr[h] = states[h]
        y = _dot(mix_scr[...], wo_ref[...])
        gate = mod_ref[0][2:3, :]
        o_ref[0] = x_ref[0] + gate * y


def _gla_mix(x, mod, q, k, v, z, gf, gb, gw, w_out, tb):
    bsz, s, d = x.shape
    nt = s // tb
    const2 = lambda b, j: (0, 0)
    both = lambda b, j: (b, jnp.where(j < nt, nt - 1 - j, j - nt), 0)
    fwd_only = lambda b, j: (b, jnp.maximum(j - nt, 0), 0)
    return pl.pallas_call(
        functools.partial(_gla_kernel, nt=nt),
        grid=(bsz, 2 * nt),
        in_specs=[
            pl.BlockSpec((1, tb, d), fwd_only),
            pl.BlockSpec((1, 3, d), lambda b, j: (b, 0, 0)),
            pl.BlockSpec((1, tb, GLA_KEY_WIDTH), fwd_only),
            pl.BlockSpec((1, tb, GLA_KEY_WIDTH), both),
            pl.BlockSpec((1, tb, GLA_VAL_WIDTH), both),
            pl.BlockSpec((1, tb, GLA_VAL_WIDTH), fwd_only),
            pl.BlockSpec((1, tb, GLA_KEY_WIDTH), fwd_only),
            pl.BlockSpec((1, tb, GLA_KEY_WIDTH), lambda b, j: (b, jnp.maximum(nt - 1 - j, 0), 0)),
            pl.BlockSpec((2, GLA_CHUNK, GLA_CHUNK), lambda b, j: (0, 0, 0)),
            pl.BlockSpec((1, GLA_DV), const2),
            pl.BlockSpec(w_out.shape, const2),
        ],
        out_specs=pl.BlockSpec((1, tb, d), fwd_only),
        out_shape=jax.ShapeDtypeStruct((bsz, s, d), F32),
        scratch_shapes=[
            pltpu.VMEM((GLA_HEADS, GLA_DK, GLA_DV), F32),
            pltpu.VMEM((GLA_HEADS, GLA_DK, GLA_DV), F32),
            pltpu.VMEM((s // GLA_CHUNK, GLA_HEADS, GLA_DK, GLA_DV), BF16),
            pltpu.VMEM((s, GLA_KEY_WIDTH), F32),
            pltpu.VMEM((GLA_HEADS, 2 * GLA_CHUNK, 2 * GLA_DK), BF16),
            pltpu.VMEM((tb, d), BF16),
        ],
        compiler_params=pltpu.CompilerParams(
            dimension_semantics=("parallel", "arbitrary"), vmem_limit_bytes=VMEM_LIMIT),
        name="gla_mix",
    )(x, mod, q, k, v, z, gf, gb, _cumsum_mats(), gw, w_out)


def kernel(x, c, norm_w, w_ada, b_ada, w_in_a, w_pool, pool_scale, q_norm_w, k_norm_w, attn_sink,
           w_out_a, w_in_c, w_gate_up, b_gate, gla_norm_w, w_out_c):
    bsz, s, d = x.shape
    assert d == D_MODEL and s % 512 == 0
    mod_all = _modulation(c, w_ada, b_ada).reshape(DEPTH, bsz, 3, d)
    bias = _attn_bias()
    band = _pool_band()
    icnt = _pool_inv_count(s)
    zeros_g = jnp.zeros((GLA_GATE_RANK, GLA_KEY_WIDTH), F32)
    for l in range(DEPTH):
        mod = mod_all[l]
        nw = norm_w[l].reshape(1, d)
        if l % 2 == 0:
            i = l // 2
            qwt = jnp.broadcast_to(
                jnp.tile(q_norm_w[i] * (LOG2E * ATT_HEAD_DIM ** -0.5), ATT_HEADS)[:, None],
                (ATT_WIDTH, SUB_ROWS))
            kw2 = jnp.tile(k_norm_w[i], 2).reshape(1, LANES)
            sink = jnp.repeat(LOG2E * attn_sink[i].reshape(ATT_KV_HEADS, ATT_GROUP), BLOCK, axis=1)
            u, zp, qt, k, vt, za = _even_in_proj(
                x, mod, nw, _pool_fold(w_in_a, w_pool, i), qwt, kw2, tm=2048)
            x = _even_mix(x, mod, u, zp, qt, k, vt, za, band, icnt,
                          pool_scale[i].reshape(1, POOL_WIDTH), bias, sink,
                          w_out_a[i].astype(BF16), tq=1024)
        else:
            jdx = l // 2
            w_g = jnp.concatenate([
                jnp.concatenate([w_gate_up[jdx, 0], zeros_g], axis=1),
                jnp.concatenate([zeros_g, w_gate_up[jdx, 1]], axis=1)], axis=0)
            w_g = (LOG2E * w_g).astype(BF16)
            b_g = LOG2E * b_gate[jdx].reshape(1, 2 * GLA_KEY_WIDTH)
            q, k, v, z, gf, gb = _odd_in_proj(x, mod, nw, w_in_c[jdx].astype(BF16), w_g, b_g, tm=1024)
            x = _gla_mix(x, mod, q, k, v, z, gf, gb, gla_norm_w[jdx].reshape(1, GLA_DV),
                         w_out_c[jdx].astype(BF16), tb=1024)
    return x
```
